```python
import math
import jax
import jax.numpy as jnp
from jax import lax
import numpy as np

D_MODEL = 1024
BATCH = 16
SEQ = 2048
DEPTH = 4

GRID_W = 64
CTX_LEN = 256
N_MIXERS = 3
N_LAYERS_A = (DEPTH + 2) // N_MIXERS
N_LAYERS_B = (DEPTH + 1) // N_MIXERS
N_LAYERS_C = DEPTH // N_MIXERS
N_MOD = 6
HEAD_DIM = 64
N_HEADS = D_MODEL // (2 * HEAD_DIM)
V_HEAD_DIM = 2 * HEAD_DIM
Q_BLOCK = 128
ROPE_THETA = 10000.0
D_FF = 4 * D_MODEL
HYENA_EMB_DIM = 33
HYENA_FILTER_ORDER = 64
HYENA_SHORT_WIDTH = 3
HYENA_DECAY_TARGET = 1e-2
HYENA_FAST_DECAY_PCT = 0.3
HYENA_SLOW_DECAY_PCT = 1.5
CONV_WIDTH = 31
NORM_EPS = 1e-6
LN_EPS = 1e-5

kernel_name = "hybrid_diffattn_hyena_conformer_dit"


def rms_norm(x, g, eps=NORM_EPS):
    xf = x.astype(jnp.float32)
    y = xf * lax.rsqrt(jnp.mean(xf * xf, axis=-1, keepdims=True) + eps)
    return (y * g.astype(jnp.float32)).astype(x.dtype)


def layer_norm(x, g, b, eps=LN_EPS):
    xf = x.astype(jnp.float32)
    mu = jnp.mean(xf, axis=-1, keepdims=True)
    var = jnp.mean(jnp.square(xf - mu), axis=-1, keepdims=True)
    y = (xf - mu) * lax.rsqrt(var + eps)
    return (y * g.astype(jnp.float32) + b.astype(jnp.float32)).astype(x.dtype)


def modulate(x, shift, scale):
    return x * (1.0 + scale) + shift


def depthwise_conv(x, w, b):
    y = lax.conv_general_dilated(
        x, w[:, None, :].astype(x.dtype), window_strides=(1,), padding="SAME",
        dimension_numbers=("NWC", "WIO", "NWC"), feature_group_count=x.shape[-1])
    return y + b


def grid_positions(n_tokens):
    rows = n_tokens // GRID_W
    row_ids = jnp.repeat(jnp.arange(rows, dtype=jnp.float32), GRID_W)
    col_ids = jnp.tile(jnp.arange(GRID_W, dtype=jnp.float32), rows)
    return row_ids, col_ids


def rope_1d(x, pos):
    half = x.shape[-1] // 2
    inv_freq = ROPE_THETA ** (-jnp.arange(half, dtype=jnp.float32) / half)
    ang = pos[:, None] * inv_freq[None, :]
    cos = jnp.cos(ang)[None, :, None, :]
    sin = jnp.sin(ang)[None, :, None, :]
    x1 = x[..., :half].astype(jnp.float32)
    x2 = x[..., half:].astype(jnp.float32)
    return jnp.concatenate([x1 * cos - x2 * sin, x2 * cos + x1 * sin], axis=-1).astype(x.dtype)


def rope_2d(x, rows, cols):
    rd = x.shape[-1] // 2
    return jnp.concatenate([rope_1d(x[..., :rd], rows), rope_1d(x[..., rd:], cols)], axis=-1)


def diff_attend(q, k, v, lam):
    s = jnp.einsum("bqhpd,bkhpd->bhpqk", q, k).astype(jnp.float32) * (HEAD_DIM ** -0.5)
    p = jax.nn.softmax(s, axis=-1)
    a = p[:, :, 0] - lam * p[:, :, 1]
    return jnp.einsum("bhqk,bkhe->bqhe", a.astype(v.dtype), v)


def differential_attention(u_lat, u_ctx, w_qkv, w_out, lam_vec, subln_g, lam_init, with_ctx):
    bsz, n_lat, _ = u_lat.shape

    def project(u):
        n = u.shape[1]
        q, k, v = jnp.split(u @ w_qkv, 3, axis=-1)
        return (q.reshape(bsz, n, N_HEADS, 2, HEAD_DIM),
                k.reshape(bsz, n, N_HEADS, 2, HEAD_DIM),
                v.reshape(bsz, n, N_HEADS, V_HEAD_DIM))

    q_l, k_l, v_l = project(u_lat)
    q_c, k_c, v_c = project(u_ctx)
    rows, cols = grid_positions(n_lat)
    q_l = rope_2d(q_l.reshape(bsz, n_lat, 2 * N_HEADS, HEAD_DIM), rows, cols).reshape(q_l.shape)
    k_l = rope_2d(k_l.reshape(bsz, n_lat, 2 * N_HEADS, HEAD_DIM), rows, cols).reshape(k_l.shape)
    lf = lam_vec.astype(jnp.float32)
    lam = jnp.exp(jnp.sum(lf[0] * lf[1])) - jnp.exp(jnp.sum(lf[2] * lf[3])) + lam_init
    k_all = jnp.concatenate([k_c, k_l], axis=1)
    v_all = jnp.concatenate([v_c, v_l], axis=1)
    n_blk = n_lat // Q_BLOCK
    q_blocks = jnp.moveaxis(q_l.reshape(bsz, n_blk, Q_BLOCK, N_HEADS, 2, HEAD_DIM), 1, 0)
    o_l = lax.map(lambda qb: diff_attend(qb, k_all, v_all, lam), q_blocks)
    o_l = jnp.moveaxis(o_l, 0, 1).reshape(bsz, n_lat, N_HEADS, V_HEAD_DIM)

    def merge_heads(o):
        o = rms_norm(o, subln_g) * (1.0 - lam_init)
        return o.reshape(bsz, o.shape[1], D_MODEL) @ w_out

    y_l = merge_heads(o_l)
    y_c = merge_heads(diff_attend(q_c, k_c, v_c, lam)) if with_ctx else None
    return y_l, y_c


def hyena_filters(n, w1, b1, w2, b2, freq, w_out):
    bands = (HYENA_EMB_DIM - 1) // 2
    t = jnp.linspace(0.0, 1.0, n, dtype=jnp.float32)[:, None]
    w = 2.0 * math.pi * jnp.arange(n, dtype=jnp.float32)[:, None] / n
    f = jnp.linspace(1e-4, bands - 1, bands, dtype=jnp.float32)[None, :]
    z = jnp.concatenate([t, jnp.cos(f * w), -jnp.sin(f * w)], axis=-1)
    hdn = jnp.sin(freq * (z @ w1 + b1))
    for j in range(w2.shape[0]):
        hdn = jnp.sin(freq * (hdn @ w2[j] + b2[j]))
    h = (hdn @ w_out).reshape(n, 2, D_MODEL)
    deltas = jnp.abs(jnp.linspace(math.log(HYENA_DECAY_TARGET) / HYENA_SLOW_DECAY_PCT,
                                  math.log(HYENA_DECAY_TARGET) / HYENA_FAST_DECAY_PCT,
                                  D_MODEL, dtype=jnp.float32))
    h = h * jnp.exp(-t * deltas[None, :])[:, None, :]
    return h[:, 0], h[:, 1]


def two_sided_fft_conv(v, h_fwd, h_bwd):
    n = v.shape[1]
    kern = jnp.concatenate([h_fwd, jnp.zeros((1, h_fwd.shape[1]), h_fwd.dtype), h_bwd[:0:-1]], axis=0)
    vf = jnp.fft.rfft(v.astype(jnp.float32), n=2 * n, axis=1)
    kf = jnp.fft.rfft(kern.astype(jnp.float32), n=2 * n, axis=0)
    y = jnp.fft.irfft(vf * kf[None], n=2 * n, axis=1)[:, :n]
    return y.astype(v.dtype)


def hyena_mixer(u, w_in, b_in, w_short, b_short, fw1, fb1, fw2, fb2, ffreq, fw_out, d_bias, w_out, b_out):
    z = depthwise_conv(u @ w_in + b_in, w_short, b_short)
    x0, x1, v = jnp.split(z, 3, axis=-1)
    h_fwd, h_bwd = hyena_filters(u.shape[1], fw1, fb1, fw2, fb2, ffreq, fw_out)
    v = v * x1
    v = two_sided_fft_conv(v, h_fwd, h_bwd) + v * d_bias
    return (v * x0) @ w_out + b_out


def conformer_conv(u, w_pw1, b_pw1, w_dw, b_dw, ln_g, ln_b, w_pw2, b_pw2):
    a, g = jnp.split(u @ w_pw1 + b_pw1, 2, axis=-1)
    z = depthwise_conv(a * jax.nn.sigmoid(g), w_dw, b_dw)
    z = jax.nn.silu(layer_norm(z, ln_g, ln_b))
    return z @ w_pw2 + b_pw2


def squared_relu_mlp(u, w_in, w_out):
    return jnp.square(jax.nn.relu(u @ w_in)) @ w_out


def setup_inputs(seed: int = 0) -> dict:
    key = jax.random.key(seed)
    ks = iter(jax.random.split(key, 48))
    D = D_MODEL

    def nrm(shape, scale):
        return scale * jax.random.normal(next(ks), shape, jnp.float32)

    return {
        "x": nrm((BATCH, SEQ, D), 1.0),
        "c": nrm((BATCH, D), 1.0),
        "ctx": nrm((BATCH, CTX_LEN, D), 1.0),
        "c_ctx": nrm((D,), 1.0),
        "w_mod": nrm((DEPTH, D, N_MOD * D), 0.5 * D ** -0.5),
        "b_mod": nrm((DEPTH, N_MOD * D), 0.02),
        "norm_mix_pre": 1.0 + nrm((DEPTH, D), 0.05),
        "norm_mix_post": 1.0 + nrm((DEPTH, D), 0.05),
        "norm_mlp_pre": 1.0 + nrm((DEPTH, D), 0.05),
        "norm_mlp_post": 1.0 + nrm((DEPTH, D), 0.05),
        "w_mlp_in": nrm((DEPTH, D, D_FF), D ** -0.5),
        "w_mlp_out": nrm((DEPTH, D_FF, D), D_FF ** -0.5),
        "attn_w_qkv": nrm((N_LAYERS_A, D, 3 * D), D ** -0.5),
        "attn_w_out": nrm((N_LAYERS_A, D, D), D ** -0.5),
        "attn_lambda": nrm((N_LAYERS_A, 4, HEAD_DIM), 0.1),
        "attn_subln": 1.0 + nrm((N_LAYERS_A, V_HEAD_DIM), 0.05),
        "hy_w_in": nrm((N_LAYERS_B, D, 3 * D), D ** -0.5),
        "hy_b_in": nrm((N_LAYERS_B, 3 * D), 0.02),
        "hy_w_short": nrm((N_LAYERS_B, HYENA_SHORT_WIDTH, 3 * D), HYENA_SHORT_WIDTH ** -0.5),
        "hy_b_short": nrm((N_LAYERS_B, 3 * D), 0.02),
        "hy_filt_w1": nrm((N_LAYERS_B, HYENA_EMB_DIM, HYENA_FILTER_ORDER), HYENA_EMB_DIM ** -0.5),
        "hy_filt_b1": nrm((N_LAYERS_B, HYENA_FILTER_ORDER), 0.1),
        "hy_filt_w2": nrm((N_LAYERS_B, 2, HYENA_FILTER_ORDER, HYENA_FILTER_ORDER), HYENA_FILTER_ORDER ** -0.5),
        "hy_filt_b2": nrm((N_LAYERS_B, 2, HYENA_FILTER_ORDER), 0.1),
        "hy_filt_freq": 1.0 + nrm((N_LAYERS_B, HYENA_FILTER_ORDER), 0.05),
        "hy_filt_w_out": nrm((N_LAYERS_B, HYENA_FILTER_ORDER, 2 * D), HYENA_FILTER_ORDER ** -0.5),
        "hy_bias": nrm((N_LAYERS_B, D), 1.0),
        "hy_w_out": nrm((N_LAYERS_B, D, D), D ** -0.5),
        "hy_b_out": nrm((N_LAYERS_B, D), 0.02),
        "cv_w_pw1": nrm((N_LAYERS_C, D, 2 * D), D ** -0.5),
        "cv_b_pw1": nrm((N_LAYERS_C, 2 * D), 0.02),
        "cv_w_dw": nrm((N_LAYERS_C, CONV_WIDTH, D), CONV_WIDTH ** -0.5),
        "cv_b_dw": nrm((N_LAYERS_C, D), 0.02),
        "cv_ln_g": 1.0 + nrm((N_LAYERS_C, D), 0.05),
        "cv_ln_b": nrm((N_LAYERS_C, D), 0.02),
        "cv_w_pw2": nrm((N_LAYERS_C, D, D), D ** -0.5),
        "cv_b_pw2": nrm((N_LAYERS_C, D), 0.02),
    }


def reference(x, c, ctx, c_ctx, w_mod, b_mod, norm_mix_pre, norm_mix_post, norm_mlp_pre, norm_mlp_post,
              w_mlp_in, w_mlp_out, attn_w_qkv, attn_w_out, attn_lambda, attn_subln,
              hy_w_in, hy_b_in, hy_w_short, hy_b_short, hy_filt_w1, hy_filt_b1, hy_filt_w2, hy_filt_b2,
              hy_filt_freq, hy_filt_w_out, hy_bias, hy_w_out, hy_b_out,
              cv_w_pw1, cv_b_pw1, cv_w_dw, cv_b_dw, cv_ln_g, cv_ln_b, cv_w_pw2, cv_b_pw2):
    h_lat, h_ctx = x, ctx
    silu_c = jax.nn.silu(c)
    silu_cc = jax.nn.silu(c_ctx)
    for i in range(DEPTH):
        last = i == DEPTH - 1
        kind, j = i % N_MIXERS, i // N_MIXERS
        need_ctx_out = not last
        need_ctx_in = need_ctx_out or kind == 0
        mod_l = jnp.split((silu_c @ w_mod[i] + b_mod[i])[:, None, :], N_MOD, axis=-1)
        mod_c = jnp.split(silu_cc @ w_mod[i] + b_mod[i], N_MOD, axis=-1)
        u_l = modulate(rms_norm(h_lat, norm_mix_pre[i]), mod_l[0], mod_l[1])
        u_c = modulate(rms_norm(h_ctx, norm_mix_pre[i]), mod_c[0], mod_c[1]) if need_ctx_in else None
        if kind == 0:
            y_l, y_c = differential_attention(u_l, u_c, attn_w_qkv[j], attn_w_out[j], attn_lambda[j],
                                              attn_subln[j], 0.8 - 0.6 * math.exp(-0.3 * i), need_ctx_out)
        elif kind == 1:
            hp = (hy_w_in[j], hy_b_in[j], hy_w_short[j], hy_b_short[j], hy_filt_w1[j], hy_filt_b1[j],
                  hy_filt_w2[j], hy_filt_b2[j], hy_filt_freq[j], hy_filt_w_out[j], hy_bias[j],
                  hy_w_out[j], hy_b_out[j])
            y_l = hyena_mixer(u_l, *hp)
            y_c = hyena_mixer(u_c, *hp) if need_ctx_out else None
        else:
            cp = (cv_w_pw1[j], cv_b_pw1[j], cv_w_dw[j], cv_b_dw[j], cv_ln_g[j], cv_ln_b[j],
                  cv_w_pw2[j], cv_b_pw2[j])
            y_l = conformer_conv(u_l, *cp)
            y_c = conformer_conv(u_c, *cp) if need_ctx_out else None
        h_lat = h_lat + mod_l[2] * rms_norm(y_l, norm_mix_post[i])
        v_l = modulate(rms_norm(h_lat, norm_mlp_pre[i]), mod_l[3], mod_l[4])
        h_lat = h_lat + mod_l[5] * rms_norm(squared_relu_mlp(v_l, w_mlp_in[i], w_mlp_out[i]), norm_mlp_post[i])
        if need_ctx_out:
            h_ctx = h_ctx + mod_c[2] * rms_norm(y_c, norm_mix_post[i])
            v_c = modulate(rms_norm(h_ctx, norm_mlp_pre[i]), mod_c[3], mod_c[4])
            h_ctx = h_ctx + mod_c[5] * rms_norm(squared_relu_mlp(v_c, w_mlp_in[i], w_mlp_out[i]), norm_mlp_post[i])
    return h_lat
```

```python
import functools
import math

import jax
import jax.numpy as jnp
from jax import lax
from jax.experimental import pallas as pl
from jax.experimental.pallas import tpu as pltpu

F32 = jnp.float32
BF16 = jnp.bfloat16

N_MIXERS = 3
N_MOD = 6
HEAD_DIM = 64
V_HEAD_DIM = 2 * HEAD_DIM
GRID_W = 64
ROPE_THETA = 10000.0
NORM_EPS = 1e-6
LN_EPS = 1e-5
HYENA_EMB_DIM = 33
HYENA_DECAY_TARGET = 1e-2
HYENA_FAST_DECAY_PCT = 0.3
HYENA_SLOW_DECAY_PCT = 1.5

LANES = 128
SUBLANES = 8
MXU_DIM = 256
VMEM_LIMIT_BYTES = 56 * 1024 * 1024
MOD_ROWS = 24
CONV_HALO = 16


def _params(*sem):
    return pltpu.CompilerParams(dimension_semantics=sem, vmem_limit_bytes=VMEM_LIMIT_BYTES)


def _resident(shape, index_map):
    return pl.BlockSpec(shape, index_map, pipeline_mode=pl.Buffered(1))


def _row_tile(rows, want):
    t = min(rows, want)
    assert rows % t == 0
    return t


def _rms(x, g):
    return x * lax.rsqrt(jnp.mean(x * x, axis=-1, keepdims=True) + NORM_EPS) * g


def _dot(a, b):
    return jnp.dot(a, b, preferred_element_type=F32)


def _mod_kernel(c_ref, w_ref, b_ref, o_ref):
    c = c_ref[...]
    s = (c / (1.0 + jnp.exp(-c))).astype(BF16)
    o_ref[0] = _dot(s, w_ref[0].astype(BF16)) + b_ref[0]


def _mod_vectors(cc, w_mod, b_mod):
    depth, d, n = w_mod.shape
    tn = 1024
    return pl.pallas_call(
        _mod_kernel,
        grid=(depth, n // tn),
        in_specs=[pl.BlockSpec((MOD_ROWS, d), lambda i, j: (0, 0)),
                  pl.BlockSpec((1, d, tn), lambda i, j: (i, 0, j)),
                  pl.BlockSpec((1, 1, tn), lambda i, j: (i, 0, j))],
        out_specs=pl.BlockSpec((1, MOD_ROWS, tn), lambda i, j: (i, 0, j)),
        out_shape=jax.ShapeDtypeStruct((depth, MOD_ROWS, n), F32),
        compiler_params=_params("arbitrary", "arbitrary"),
        name="mod_vectors",
    )(cc, w_mod, b_mod.reshape(depth, 1, n))


def _rope_group(z, cos, sin, first_half):
    partner = jnp.where(first_half, pltpu.roll(z, LANES - 16, axis=1), pltpu.roll(z, 16, axis=1))
    return z * cos + partner * sin


def _pre_kernel(*refs, kind, d, rope):
    h_ref, g_ref, mod_ref, w_ref = refs[:4]
    rest = list(refs[4:])
    b_ref = rest.pop(0) if kind != "attn" else None
    cos_ref, sin_ref = (rest.pop(0), rest.pop(0)) if rope else (None, None)
    (o_ref,) = rest

    u = _rms(h_ref[...], g_ref[...]) * (1.0 + mod_ref[0, 1:2, :]) + mod_ref[0, 0:1, :]
    ub = u.astype(BF16)
    nc = 2 * MXU_DIM
    if kind == "conformer":
        for c in range(d // nc):
            a = _dot(ub, w_ref[:, c * nc:(c + 1) * nc]) + b_ref[:, c * nc:(c + 1) * nc]
            g = _dot(ub, w_ref[:, d + c * nc:d + (c + 1) * nc]) + b_ref[:, d + c * nc:d + (c + 1) * nc]
            o_ref[:, c * nc:(c + 1) * nc] = (a / (1.0 + jnp.exp(-g))).astype(o_ref.dtype)
        return
    if rope:
        lane = lax.broadcasted_iota(jnp.int32, (1, LANES), 1)
        first_half = (lane % 32) < 16
        cos, sin = cos_ref[...], sin_ref[...]
    for c in range(3 * d // nc):
        z = _dot(ub, w_ref[:, c * nc:(c + 1) * nc])
        if kind == "hyena":
            z = z + b_ref[:, c * nc:(c + 1) * nc]
            o_ref[:, c * nc:(c + 1) * nc] = z.astype(o_ref.dtype)
            continue
        is_q, is_qk = c * nc < d, c * nc < 2 * d
        for gi in range(nc // LANES):
            zg = z[:, gi * LANES:(gi + 1) * LANES]
            if rope and is_qk:
                zg = _rope_group(zg, cos, sin, first_half)
            if is_q:
                zg = zg * (HEAD_DIM ** -0.5)
            o_ref[:, c * nc + gi * LANES:c * nc + (gi + 1) * LANES] = zg.astype(o_ref.dtype)


def _pre_project(h, g, mod, w, b, *, kind, rows_per_mod, rope_tabs=None, tm=512):
    rows, d = h.shape
    n_in = w.shape[1]
    n_out = d if kind == "conformer" else n_in
    tm = _row_tile(min(rows, rows_per_mod), tm)
    rope = rope_tabs is not None
    ins = [h, g.reshape(1, d), mod, w]
    specs = [pl.BlockSpec((tm, d), lambda i: (i, 0)),
             pl.BlockSpec((1, d), lambda i: (0, 0)),
             pl.BlockSpec((1, N_MOD, d), lambda i: (i * tm // rows_per_mod, 0, 0)),
             _resident((d, n_in), lambda i: (0, 0))]
    if kind != "attn":
        ins.append(b.reshape(1, n_in))
        specs.append(pl.BlockSpec((1, n_in), lambda i: (0, 0)))
    if rope:
        seq = rope_tabs[0].shape[0]
        per = seq // tm
        ins += list(rope_tabs)
        specs += [pl.BlockSpec((tm, LANES), lambda i: (i % per, 0))] * 2
    return pl.pallas_call(
        functools.partial(_pre_kernel, kind=kind, d=d, rope=rope),
        grid=(rows // tm,),
        in_specs=specs,
        out_specs=pl.BlockSpec((tm, n_out), lambda i: (i, 0)),
        out_shape=jax.ShapeDtypeStruct((rows, n_out), BF16),
        compiler_params=_params("arbitrary"),
        name=f"pre_{kind}",
    )(*ins)


def _rope_tables(seq):
    t = jnp.arange(seq)
    pos = jnp.stack([(t // GRID_W).astype(F32), (t % GRID_W).astype(F32)], axis=1)
    half = HEAD_DIM // 4
    inv_freq = ROPE_THETA ** (-jnp.arange(half, dtype=F32) / half)
    ang = pos[:, :, None] * inv_freq[None, None, :]
    cos = jnp.concatenate([jnp.cos(ang), jnp.cos(ang)], axis=-1).reshape(seq, HEAD_DIM)
    sin = jnp.concatenate([-jnp.sin(ang), jnp.sin(ang)], axis=-1).reshape(seq, HEAD_DIM)
    rep = LANES // HEAD_DIM
    return jnp.tile(cos, (1, rep)), jnp.tile(sin, (1, rep))


def _attn_kernel(*refs, n_kv, lam_init, tq):
    lam_ref, sub_ref, q_ref = refs[:3]
    k_refs = refs[3:3 + 2 * n_kv:2]
    v_refs = refs[4:4 + 2 * n_kv:2]
    o_ref = refs[-1]

    lv = lam_ref[...]
    lam = (jnp.exp(jnp.sum(lv[0:1] * lv[1:2], axis=1, keepdims=True))
           - jnp.exp(jnp.sum(lv[2:3] * lv[3:4], axis=1, keepdims=True)) + lam_init)

    q = q_ref[0]
    lane = lax.broadcasted_iota(jnp.int32, (1, V_HEAD_DIM), 1)
    zero = jnp.zeros_like(q)
    qq = jnp.concatenate([jnp.where(lane < HEAD_DIM, q, zero), jnp.where(lane >= HEAD_DIM, q, zero)], axis=0)
    s = [lax.dot_general(qq, k_ref[0], (((1,), (1,)), ((), ())), preferred_element_type=F32) for k_ref in k_refs]
    m = functools.reduce(jnp.maximum, [jnp.max(si, axis=1, keepdims=True) for si in s])
    e = [jnp.exp(si - m) for si in s]
    denom = functools.reduce(jnp.add, [jnp.sum(ei, axis=1, keepdims=True) for ei in e])
    r = 1.0 / denom
    c1 = r[:tq]
    c2 = -lam * r[tq:]
    o = functools.reduce(jnp.add, [_dot((ei[:tq] * c1 + ei[tq:] * c2).astype(BF16), v_ref[0])
                                   for ei, v_ref in zip(e, v_refs)])
    o = _rms(o, sub_ref[...]) * (1.0 - lam_init)
    o_ref[0] = o.astype(o_ref.dtype)


def _diff_attention(q_src, kv_srcs, lam_vec, subln, lam_init, d, tq=256):
    bsz, lq, _ = q_src.shape
    n_heads = d // V_HEAD_DIM
    tq = _row_tile(lq, tq)
    ins = [lam_vec, subln.reshape(1, V_HEAD_DIM), q_src]
    specs = [pl.BlockSpec(lam_vec.shape, lambda b, h, i: (0, 0)),
             pl.BlockSpec((1, V_HEAD_DIM), lambda b, h, i: (0, 0)),
             pl.BlockSpec((1, tq, V_HEAD_DIM), lambda b, h, i: (b, i, h))]
    for src in kv_srcs:
        lk = src.shape[1]
        ins += [src, src]
        specs += [pl.BlockSpec((1, lk, V_HEAD_DIM), lambda b, h, i: (b, 0, n_heads + h)),
                  pl.BlockSpec((1, lk, V_HEAD_DIM), lambda b, h, i: (b, 0, 2 * n_heads + h))]
    return pl.pallas_call(
        functools.partial(_attn_kernel, n_kv=len(kv_srcs), lam_init=lam_init, tq=tq),
        grid=(bsz, n_heads, lq // tq),
        in_specs=specs,
        out_specs=pl.BlockSpec((1, tq, V_HEAD_DIM), lambda b, h, i: (b, i, h)),
        out_shape=jax.ShapeDtypeStruct((bsz, lq, d), BF16),
        compiler_params=_params("arbitrary", "arbitrary", "arbitrary"),
        name="diff_attention",
    )(*ins)


def _post_kernel(*refs, has_bias):
    y_ref, w_ref = refs[:2]
    rest = list(refs[2:])
    b_ref = rest.pop(0) if has_bias else None
    h_ref, g_ref, mod_ref, o_ref = rest
    y = _dot(y_ref[...], w_ref[...])
    if has_bias:
        y = y + b_ref[...]
    o_ref[...] = h_ref[...] + mod_ref[0, 2:3, :] * _rms(y, g_ref[...])


def _post_project(y, w, b, h, g, mod, *, rows_per_mod, tm=512):
    rows, d = h.shape
    tm = _row_tile(min(rows, rows_per_mod), tm)
    has_bias = b is not None
    ins = [y, w] + ([b.reshape(1, d)] if has_bias else []) + [h, g.reshape(1, d), mod]
    specs = ([pl.BlockSpec((tm, d), lambda i: (i, 0)), _resident((d, d), lambda i: (0, 0))]
             + ([pl.BlockSpec((1, d), lambda i: (0, 0))] if has_bias else [])
             + [pl.BlockSpec((tm, d), lambda i: (i, 0)),
                pl.BlockSpec((1, d), lambda i: (0, 0)),
                pl.BlockSpec((1, N_MOD, d), lambda i: (i * tm // rows_per_mod, 0, 0))])
    return pl.pallas_call(
        functools.partial(_post_kernel, has_bias=has_bias),
        grid=(rows // tm,),
        in_specs=specs,
        out_specs=pl.BlockSpec((tm, d), lambda i: (i, 0)),
        out_shape=jax.ShapeDtypeStruct((rows, d), F32),
        compiler_params=_params("arbitrary"),
        name="post_project",
    )(*ins)


def _mlp_kernel(h_ref, gpre_ref, gpost_ref, mod_ref, w1_ref, w2_ref, o_ref, hid_ref):
    x = h_ref[...]
    v = (_rms(x, gpre_ref[...]) * (1.0 + mod_ref[0, 4:5, :]) + mod_ref[0, 3:4, :]).astype(BF16)
    dff = w1_ref.shape[1]
    fc = 4 * MXU_DIM
    for c in range(dff // fc):
        hid = jnp.maximum(_dot(v, w1_ref[:, c * fc:(c + 1) * fc]), 0.0)
        hid_ref[:, c * fc:(c + 1) * fc] = (hid * hid).astype(BF16)
    y = _dot(hid_ref[...], w2_ref[...])
    o_ref[...] = x + mod_ref[0, 5:6, :] * _rms(y, gpost_ref[...])


def _mlp_sublayer(h, gpre, gpost, mod, w1, w2, *, rows_per_mod, tm=512):
    rows, d = h.shape
    dff = w1.shape[1]
    tm = _row_tile(min(rows, rows_per_mod), tm)
    return pl.pallas_call(
        _mlp_kernel,
        grid=(rows // tm,),
        in_specs=[pl.BlockSpec((tm, d), lambda i: (i, 0)),
                  pl.BlockSpec((1, d), lambda i: (0, 0)),
                  pl.BlockSpec((1, d), lambda i: (0, 0)),
                  pl.BlockSpec((1, N_MOD, d), lambda i: (i * tm // rows_per_mod, 0, 0)),
                  _resident((d, dff), lambda i: (0, 0)),
                  _resident((dff, d), lambda i: (0, 0))],
        out_specs=pl.BlockSpec((tm, d), lambda i: (i, 0)),
        out_shape=jax.ShapeDtypeStruct((rows, d), F32),
        scratch_shapes=[pltpu.VMEM((tm, dff), BF16)],
        compiler_params=_params("arbitrary"),
        name="mlp_sublayer",
    )(h, gpre.reshape(1, d), gpost.reshape(1, d), mod, w1, w2)


def _hyena_filter_kernel(z_ref, w1_ref, b1_ref, w2_ref, b2_ref, fr_ref, wo_ref, dl_ref, hf_ref, hb_ref):
    hi = lax.Precision.HIGHEST
    z = z_ref[...]
    freq = fr_ref[...]
    hdn = jnp.sin(freq * (jnp.dot(z, w1_ref[...], precision=hi, preferred_element_type=F32) + b1_ref[...]))
    for j in range(w2_ref.shape[0]):
        hdn = jnp.sin(freq * (jnp.dot(hdn, w2_ref[j], precision=hi, preferred_element_type=F32) + b2_ref[j]))
    h = jnp.dot(hdn, wo_ref[...], precision=hi, preferred_element_type=F32)
    d = hf_ref.shape[1]
    window = jnp.exp(-z[:, 0:1] * dl_ref[...])
    hf_ref[...] = h[:, :d] * window
    hb_ref[...] = h[:, d:] * window


def _hyena_filters(n, w1, b1, w2, b2, freq, w_out, d):
    bands = (HYENA_EMB_DIM - 1) // 2
    t = jnp.linspace(0.0, 1.0, n, dtype=F32)[:, None]
    w = 2.0 * math.pi * jnp.arange(n, dtype=F32)[:, None] / n
    f = jnp.linspace(1e-4, bands - 1, bands, dtype=F32)[None, :]
    z = jnp.concatenate([t, jnp.cos(f * w), -jnp.sin(f * w),
                         jnp.zeros((n, LANES - HYENA_EMB_DIM), F32)], axis=-1)
    w1p = jnp.concatenate([w1, jnp.zeros((LANES - HYENA_EMB_DIM, w1.shape[1]), F32)], axis=0)
    deltas = jnp.abs(jnp.linspace(math.log(HYENA_DECAY_TARGET) / HYENA_SLOW_DECAY_PCT,
                                  math.log(HYENA_DECAY_TARGET) / HYENA_FAST_DECAY_PCT, d, dtype=F32))
    order = w1.shape[1]
    tr = _row_tile(n, 256)
    full = lambda *shape: pl.BlockSpec(shape, lambda i: (0,) * len(shape))
    return pl.pallas_call(
        _hyena_filter_kernel,
        grid=(n // tr,),
        in_specs=[pl.BlockSpec((tr, LANES), lambda i: (i, 0)),
                  full(LANES, order), full(1, order), full(*w2.shape), full(w2.shape[0], 1, order),
                  full(1, order), full(order, 2 * d), full(1, d)],
        out_specs=[pl.BlockSpec((tr, d), lambda i: (i, 0))] * 2,
        out_shape=[jax.ShapeDtypeStruct((n, d), F32)] * 2,
        compiler_params=_params("arbitrary"),
        name="hyena_filters",
    )(z, w1p, b1.reshape(1, order), w2, b2.reshape(w2.shape[0], 1, order), freq.reshape(1, order),
      w_out, deltas.reshape(1, d))


def _dft_matrices(n):
    k = jnp.arange(n, dtype=jnp.int32)
    m = (k[:, None] * k[None, :]) % (2 * n)
    ang = m.astype(F32) * (math.pi / n)
    return jnp.cos(ang).astype(BF16), jnp.sin(ang).astype(BF16)


def _alt_sign(n):
    row = lax.broadcasted_iota(jnp.int32, (n, 1), 0)
    return jnp.where(row % 2 == 0, 1.0, -1.0).astype(F32), row


def _hyena_spec_kernel(c_ref, s_ref, hf_ref, k2_ref, kr_ref, ks_ref, kn_ref):
    n = hf_ref.shape[0]
    sgn, row = _alt_sign(n)
    hf, k2 = hf_ref[...], k2_ref[...]
    hfb, k2b = hf.astype(BF16), k2.astype(BF16)
    wk = jnp.where(row == 0, 0.5 / n, 1.0 / n)
    kr_ref[...] = (_dot(c_ref[...], hfb) + sgn * _dot(c_ref[...], k2b)) * wk
    ks_ref[...] = (_dot(s_ref[...], hfb) + sgn * _dot(s_ref[...], k2b)) * wk
    kn_ref[...] = jnp.sum(sgn * (hf + k2), axis=0, keepdims=True) * (0.5 / n)


def _hyena_spectrum(cmat, smat, hf, hb):
    n, d = hf.shape
    k2 = jnp.concatenate([jnp.zeros((1, d), F32), jnp.flip(hb[1:], axis=0)], axis=0)
    cb = min(d, MXU_DIM)
    return pl.pallas_call(
        _hyena_spec_kernel,
        grid=(d // cb,),
        in_specs=[_resident((n, n), lambda j: (0, 0)), _resident((n, n), lambda j: (0, 0)),
                  pl.BlockSpec((n, cb), lambda j: (0, j)), pl.BlockSpec((n, cb), lambda j: (0, j))],
        out_specs=[pl.BlockSpec((n, cb), lambda j: (0, j)), pl.BlockSpec((n, cb), lambda j: (0, j)),
                   pl.BlockSpec((1, cb), lambda j: (0, j))],
        out_shape=[jax.ShapeDtypeStruct((n, d), F32), jax.ShapeDtypeStruct((n, d), F32),
                   jax.ShapeDtypeStruct((1, d), F32)],
        compiler_params=_params("arbitrary"),
        name="hyena_spectrum",
    )(cmat, smat, hf, k2)


def _hyena_conv_kernel(zx0_ref, zx1_ref, zv_ref, w0_ref, w1_ref, wv_ref, b0_ref, b1_ref, bv_ref,
                       c_ref, s_ref, kr_ref, ks_ref, kn_ref, db_ref, o_ref,
                       x0_scr, vv_scr, vb_scr, p_scr, q_scr):
    n = o_ref.shape[1]
    sgn, row = _alt_sign(n)

    def short_conv(z_ref, w_ref, b_ref):
        x = z_ref[0].astype(F32)
        prev = jnp.where(row == 0, 0.0, pltpu.roll(x, 1, axis=0))
        nxt = jnp.where(row == n - 1, 0.0, pltpu.roll(x, n - 1, axis=0))
        return w_ref[0:1, :] * prev + w_ref[1:2, :] * x + w_ref[2:3, :] * nxt + b_ref[...]

    x0_scr[...] = short_conv(zx0_ref, w0_ref, b0_ref)
    vv = short_conv(zv_ref, wv_ref, bv_ref) * short_conv(zx1_ref, w1_ref, b1_ref)
    vv_scr[...] = vv
    vb_scr[...] = vv.astype(BF16)
    nyq = jnp.sum(vv * sgn, axis=0, keepdims=True) * kn_ref[...]

    ch = min(n, 2 * MXU_DIM)
    for f in range(n // ch):
        sl = slice(f * ch, (f + 1) * ch)
        fr = _dot(c_ref[sl, :], vb_scr[...])
        fs = _dot(s_ref[sl, :], vb_scr[...])
        kr, ks = kr_ref[sl, :], ks_ref[sl, :]
        p_scr[sl, :] = (fr * kr - fs * ks).astype(BF16)
        q_scr[sl, :] = (fr * ks + fs * kr).astype(BF16)
    for t in range(n // ch):
        sl = slice(t * ch, (t + 1) * ch)
        y = _dot(c_ref[sl, :], p_scr[...]) + _dot(s_ref[sl, :], q_scr[...])
        y = y + sgn[sl, :] * nyq + vv_scr[sl, :] * db_ref[...]
        o_ref[0, sl, :] = (y * x0_scr[sl, :]).astype(o_ref.dtype)


def _hyena_conv(z, w_short, b_short, cmat, smat, kr, ks, kn, d_bias):
    bsz, n, d3 = z.shape
    d = d3 // 3
    cb = min(d, MXU_DIM)
    nb = d // cb
    zspec = lambda g: pl.BlockSpec((1, n, cb), lambda j, b: (b, 0, g * nb + j))
    wspec = lambda g: pl.BlockSpec((w_short.shape[0], cb), lambda j, b: (0, g * nb + j))
    bspec = lambda g: pl.BlockSpec((1, cb), lambda j, b: (0, g * nb + j))
    col = pl.BlockSpec((n, cb), lambda j, b: (0, j))
    vec = pl.BlockSpec((1, cb), lambda j, b: (0, j))
    bs2 = b_short.reshape(1, d3)
    return pl.pallas_call(
        _hyena_conv_kernel,
        grid=(nb, bsz),
        in_specs=[zspec(0), zspec(1), zspec(2), wspec(0), wspec(1), wspec(2), bspec(0), bspec(1), bspec(2),
                  _resident((n, n), lambda j, b: (0, 0)), _resident((n, n), lambda j, b: (0, 0)),
                  col, col, vec, vec],
        out_specs=pl.BlockSpec((1, n, cb), lambda j, b: (b, 0, j)),
        out_shape=jax.ShapeDtypeStruct((bsz, n, d), BF16),
        scratch_shapes=[pltpu.VMEM((n, cb), F32), pltpu.VMEM((n, cb), F32), pltpu.VMEM((n, cb), BF16),
                        pltpu.VMEM((n, cb), BF16), pltpu.VMEM((n, cb), BF16)],
        compiler_params=_params("arbitrary", "arbitrary"),
        name="hyena_conv",
    )(z, z, z, w_short, w_short, w_short, bs2, bs2, bs2, cmat, smat, kr, ks, kn, d_bias.reshape(1, d))


def _conformer_post_kernel(x_ref, wd_ref, bd_ref, lg_ref, lb_ref, w_ref, b_ref, h_ref, g_ref, mod_ref, o_ref,
                           xp_scr, cv_scr, *, width):
    n, d = x_ref.shape[1], x_ref.shape[2]
    tm = o_ref.shape[0]
    j = pl.program_id(1)

    @pl.when(j == 0)
    def _():
        xp_scr[0:CONV_HALO, :] = jnp.zeros((CONV_HALO, d), F32)
        xp_scr[CONV_HALO:CONV_HALO + n, :] = x_ref[0].astype(F32)
        xp_scr[CONV_HALO + n:, :] = jnp.zeros((CONV_HALO, d), F32)

    base = pl.multiple_of(j * tm, tm)
    lc = MXU_DIM
    reach = width // 2
    for c in range(d // lc):
        cs = slice(c * lc, (c + 1) * lc)
        slab = xp_scr[pl.ds(base, tm + 2 * CONV_HALO), cs]
        acc = jnp.zeros((tm, lc), F32) + bd_ref[:, cs]
        for r in range(SUBLANES):
            sr = slab if r == 0 else pltpu.roll(slab, r, axis=0)
            for a in range(0, 2 * CONV_HALO // SUBLANES + 1):
                k = SUBLANES * a - r - (CONV_HALO - reach)
                if 0 <= k < width:
                    acc = acc + wd_ref[k:k + 1, cs] * sr[SUBLANES * a:SUBLANES * a + tm, :]
        cv_scr[:, cs] = acc

    z = cv_scr[...]
    mu = jnp.mean(z, axis=-1, keepdims=True)
    zc = z - mu
    var = jnp.mean(zc * zc, axis=-1, keepdims=True)
    zn = zc * lax.rsqrt(var + LN_EPS) * lg_ref[...] + lb_ref[...]
    act = (zn / (1.0 + jnp.exp(-zn))).astype(BF16)
    y = _dot(act, w_ref[...]) + b_ref[...]
    o_ref[...] = h_ref[...] + mod_ref[0, 2:3, :] * _rms(y, g_ref[...])


def _conformer_post(x, w_dw, b_dw, ln_g, ln_b, w, b, h, g, mod, *, per_batch_mod, tm=256):
    bsz, n, d = x.shape
    width = w_dw.shape[0]
    assert width // 2 < CONV_HALO
    tm = _row_tile(n, tm)
    per = n // tm
    vec = pl.BlockSpec((1, d), lambda bi, j: (0, 0))
    return pl.pallas_call(
        functools.partial(_conformer_post_kernel, width=width),
        grid=(bsz, per),
        in_specs=[pl.BlockSpec((1, n, d), lambda bi, j: (bi, 0, 0)),
                  pl.BlockSpec((width, d), lambda bi, j: (0, 0)), vec, vec, vec,
                  _resident((d, d), lambda bi, j: (0, 0)), vec,
                  pl.BlockSpec((tm, d), lambda bi, j: (bi * per + j, 0)), vec,
                  pl.BlockSpec((1, N_MOD, d), lambda bi, j: (bi if per_batch_mod else 0, 0, 0))],
        out_specs=pl.BlockSpec((tm, d), lambda bi, j: (bi * per + j, 0)),
        out_shape=jax.ShapeDtypeStruct((bsz * n, d), F32),
        scratch_shapes=[pltpu.VMEM((n + 2 * CONV_HALO, d), F32), pltpu.VMEM((tm, d), F32)],
        compiler_params=_params("arbitrary", "arbitrary"),
        name="conformer_post",
    )(x, w_dw, b_dw.reshape(1, d), ln_g.reshape(1, d), ln_b.reshape(1, d), w, b.reshape(1, d), h,
      g.reshape(1, d), mod)


def kernel(x, c, ctx, c_ctx, w_mod, b_mod, norm_mix_pre, norm_mix_post, norm_mlp_pre, norm_mlp_post, w_mlp_in, w_mlp_out, attn_w_qkv, attn_w_out, attn_lambda, attn_subln, hy_w_in, hy_b_in, hy_w_short, hy_b_short, hy_filt_w1, hy_filt_b1, hy_filt_w2, hy_filt_b2, hy_filt_freq, hy_filt_w_out, hy_bias, hy_w_out, hy_b_out, cv_w_pw1, cv_b_pw1, cv_w_dw, cv_b_dw, cv_ln_g, cv_ln_b, cv_w_pw2, cv_b_pw2):
    bsz, seq, d = x.shape
    n_ctx = ctx.shape[1]
    depth = w_mod.shape[0]
    assert bsz + 1 <= MOD_ROWS and d % (2 * MXU_DIM) == 0

    cc = jnp.concatenate([c, c_ctx[None, :], jnp.zeros((MOD_ROWS - bsz - 1, d), F32)], axis=0)
    mods = _mod_vectors(cc, w_mod, b_mod)
    rope_tabs = _rope_tables(seq)

    h_lat = x.reshape(bsz * seq, d)
    h_ctx = ctx.reshape(bsz * n_ctx, d)
    lat = dict(rows_per_mod=seq)
    cx = dict(rows_per_mod=bsz * n_ctx)
    for i in range(depth):
        last = i == depth - 1
        kind, j = i % N_MIXERS, i // N_MIXERS
        ctx_out = not last
        mod_l = mods[i, :bsz].reshape(bsz, N_MOD, d)
        mod_c = mods[i, bsz:bsz + 1].reshape(1, N_MOD, d)
        g_pre, g_post = norm_mix_pre[i], norm_mix_post[i]
        if kind == 0:
            lam_init = 0.8 - 0.6 * math.exp(-0.3 * i)
            w_qkv, w_o = attn_w_qkv[j].astype(BF16), attn_w_out[j].astype(BF16)
            z_l = _pre_project(h_lat, g_pre, mod_l, w_qkv, None, kind="attn", rope_tabs=rope_tabs, **lat)
            z_c = _pre_project(h_ctx, g_pre, mod_c, w_qkv, None, kind="attn", **cx)
            z_l = z_l.reshape(bsz, seq, 3 * d)
            z_c = z_c.reshape(bsz, n_ctx, 3 * d)
            o_l = _diff_attention(z_l, [z_c, z_l], attn_lambda[j], attn_subln[j], lam_init, d)
            h_lat = _post_project(o_l.reshape(bsz * seq, d), w_o, None, h_lat, g_post, mod_l, **lat)
            if ctx_out:
                o_c = _diff_attention(z_c, [z_c], attn_lambda[j], attn_subln[j], lam_init, d)
                h_ctx = _post_project(o_c.reshape(bsz * n_ctx, d), w_o, None, h_ctx, g_post, mod_c, **cx)
        elif kind == 1:
            w_in, w_o = hy_w_in[j].astype(BF16), hy_w_out[j].astype(BF16)

            def hyena(h, mod, n, rpm):
                z = _pre_project(h, g_pre, mod, w_in, hy_b_in[j], kind="hyena", rows_per_mod=rpm)
                hf, hb = _hyena_filters(n, hy_filt_w1[j], hy_filt_b1[j], hy_filt_w2[j], hy_filt_b2[j],
                                        hy_filt_freq[j], hy_filt_w_out[j], d)
                cmat, smat = _dft_matrices(n)
                kr, ks, kn = _hyena_spectrum(cmat, smat, hf, hb)
                g = _hyena_conv(z.reshape(bsz, n, 3 * d), hy_w_short[j], hy_b_short[j], cmat, smat,
                                kr, ks, kn, hy_bias[j])
                return _post_project(g.reshape(bsz * n, d), w_o, hy_b_out[j], h, g_post, mod, rows_per_mod=rpm)

            h_lat = hyena(h_lat, mod_l, seq, seq)
            if ctx_out:
                h_ctx = hyena(h_ctx, mod_c, n_ctx, bsz * n_ctx)
        else:
            w_1, w_2 = cv_w_pw1[j].astype(BF16), cv_w_pw2[j].astype(BF16)

            def conformer(h, mod, n, rpm, per_batch_mod):
                a = _pre_project(h, g_pre, mod, w_1, cv_b_pw1[j], kind="conformer", rows_per_mod=rpm)
                return _conformer_post(a.reshape(bsz, n, d), cv_w_dw[j], cv_b_dw[j], cv_ln_g[j], cv_ln_b[j],
                                       w_2, cv_b_pw2[j], h, g_post, mod, per_batch_mod=per_batch_mod)

            h_lat = conformer(h_lat, mod_l, seq, seq, True)
            if ctx_out:
                h_ctx = conformer(h_ctx, mod_c, n_ctx, bsz * n_ctx, False)
        w1, w2 = w_mlp_in[i].astype(BF16), w_mlp_out[i].astype(BF16)
        h_lat = _mlp_sublayer(h_lat, norm_mlp_pre[i], norm_mlp_post[i], mod_l, w1, w2, **lat)
        if ctx_out:
            h_ctx = _mlp_sublayer(h_ctx, norm_mlp_pre[i], norm_mlp_post[i], mod_c, w1, w2, **cx)
    return h_lat.reshape(bsz, seq, d)
```

```python
import functools
import math

import jax
import jax.numpy as jnp
from jax import lax
from jax.experimental import pallas as pl
from jax.experimental.pallas import tpu as pltpu

F32 = jnp.float32
BF16 = jnp.bfloat16

N_MIXERS = 3
N_MOD = 6
HEAD_DIM = 64
V_HEAD_DIM = 2 * HEAD_DIM
GRID_W = 64
ROPE_THETA = 10000.0
NORM_EPS = 1e-6
LN_EPS = 1e-5
HYENA_EMB_DIM = 33
HYENA_DECAY_TARGET = 1e-2
HYENA_FAST_DECAY_PCT = 0.3
HYENA_SLOW_DECAY_PCT = 1.5

LANES = 128
SUBLANES = 8
MXU_DIM = 256
VMEM_LIMIT_BYTES = 56 * 1024 * 1024
MOD_ROWS = 24
CONV_HALO = 16


def _params(*sem):
    return pltpu.CompilerParams(dimension_semantics=sem, vmem_limit_bytes=VMEM_LIMIT_BYTES)


def _resident(shape, index_map):
    return pl.BlockSpec(shape, index_map, pipeline_mode=pl.Buffered(1))


def _row_tile(rows, want):
    t = min(rows, want)
    assert rows % t == 0
    return t


def _rms(x, g):
    return x * lax.rsqrt(jnp.mean(x * x, axis=-1, keepdims=True) + NORM_EPS) * g


def _dot(a, b):
    return jnp.dot(a, b, preferred_element_type=F32)


def _mod_kernel(c_ref, w_ref, b_ref, o_ref):
    c = c_ref[...]
    s = (c / (1.0 + jnp.exp(-c))).astype(BF16)
    o_ref[0] = _dot(s, w_ref[0].astype(BF16)) + b_ref[0]


def _mod_vectors(cc, w_mod, b_mod):
    depth, d, n = w_mod.shape
    tn = 1024
    return pl.pallas_call(
        _mod_kernel,
        grid=(depth, n // tn),
        in_specs=[pl.BlockSpec((MOD_ROWS, d), lambda i, j: (0, 0)),
                  pl.BlockSpec((1, d, tn), lambda i, j: (i, 0, j)),
                  pl.BlockSpec((1, 1, tn), lambda i, j: (i, 0, j))],
        out_specs=pl.BlockSpec((1, MOD_ROWS, tn), lambda i, j: (i, 0, j)),
        out_shape=jax.ShapeDtypeStruct((depth, MOD_ROWS, n), F32),
        compiler_params=_params("arbitrary", "arbitrary"),
        name="mod_vectors",
    )(cc, w_mod, b_mod.reshape(depth, 1, n))


def _rope_group(z, cos, sin, first_half):
    partner = jnp.where(first_half, pltpu.roll(z, LANES - 16, axis=1), pltpu.roll(z, 16, axis=1))
    return z * cos + partner * sin


def _pre_kernel(*refs, kind, d, rope):
    h_ref, g_ref, mod_ref, w_ref = refs[:4]
    rest = list(refs[4:])
    b_ref = rest.pop(0) if kind != "attn" else None
    cos_ref, sin_ref = (rest.pop(0), rest.pop(0)) if rope else (None, None)
    (o_ref,) = rest

    u = _rms(h_ref[...], g_ref[...]) * (1.0 + mod_ref[0, 1:2, :]) + mod_ref[0, 0:1, :]
    ub = u.astype(BF16)
    nc = 2 * MXU_DIM
    if kind == "conformer":
        for c in range(d // nc):
            a = _dot(ub, w_ref[:, c * nc:(c + 1) * nc]) + b_ref[:, c * nc:(c + 1) * nc]
            g = _dot(ub, w_ref[:, d + c * nc:d + (c + 1) * nc]) + b_ref[:, d + c * nc:d + (c + 1) * nc]
            o_ref[:, c * nc:(c + 1) * nc] = (a / (1.0 + jnp.exp(-g))).astype(o_ref.dtype)
        return
    if rope:
        lane = lax.broadcasted_iota(jnp.int32, (1, LANES), 1)
        first_half = (lane % 32) < 16
        cos, sin = cos_ref[...], sin_ref[...]
    for c in range(3 * d // nc):
        z = _dot(ub, w_ref[:, c * nc:(c + 1) * nc])
        if kind == "hyena":
            z = z + b_ref[:, c * nc:(c + 1) * nc]
            o_ref[:, c * nc:(c + 1) * nc] = z.astype(o_ref.dtype)
            continue
        is_q, is_qk = c * nc < d, c * nc < 2 * d
        for gi in range(nc // LANES):
            zg = z[:, gi * LANES:(gi + 1) * LANES]
            if rope and is_qk:
                zg = _rope_group(zg, cos, sin, first_half)
            if is_q:
                zg = zg * (HEAD_DIM ** -0.5 * math.log2(math.e))
            o_ref[:, c * nc + gi * LANES:c * nc + (gi + 1) * LANES] = zg.astype(o_ref.dtype)


def _pre_project(h, g, mod, w, b, *, kind, rows_per_mod, rope_tabs=None, tm=512):
    rows, d = h.shape
    n_in = w.shape[1]
    n_out = d if kind == "conformer" else n_in
    tm = _row_tile(min(rows, rows_per_mod), tm)
    rope = rope_tabs is not None
    ins = [h, g.reshape(1, d), mod, w]
    specs = [pl.BlockSpec((tm, d), lambda i: (i, 0)),
             pl.BlockSpec((1, d), lambda i: (0, 0)),
             pl.BlockSpec((1, N_MOD, d), lambda i: (i * tm // rows_per_mod, 0, 0)),
             _resident((d, n_in), lambda i: (0, 0))]
    if kind != "attn":
        ins.append(b.reshape(1, n_in))
        specs.append(pl.BlockSpec((1, n_in), lambda i: (0, 0)))
    if rope:
        seq = rope_tabs[0].shape[0]
        per = seq // tm
        ins += list(rope_tabs)
        specs += [pl.BlockSpec((tm, LANES), lambda i: (i % per, 0))] * 2
    return pl.pallas_call(
        functools.partial(_pre_kernel, kind=kind, d=d, rope=rope),
        grid=(rows // tm,),
        in_specs=specs,
        out_specs=pl.BlockSpec((tm, n_out), lambda i: (i, 0)),
        out_shape=jax.ShapeDtypeStruct((rows, n_out), BF16),
        compiler_params=_params("arbitrary"),
        name=f"pre_{kind}",
    )(*ins)


def _rope_tables(seq):
    t = jnp.arange(seq)
    pos = jnp.stack([(t // GRID_W).astype(F32), (t % GRID_W).astype(F32)], axis=1)
    half = HEAD_DIM // 4
    inv_freq = ROPE_THETA ** (-jnp.arange(half, dtype=F32) / half)
    ang = pos[:, :, None] * inv_freq[None, None, :]
    cos = jnp.concatenate([jnp.cos(ang), jnp.cos(ang)], axis=-1).reshape(seq, HEAD_DIM)
    sin = jnp.concatenate([-jnp.sin(ang), jnp.sin(ang)], axis=-1).reshape(seq, HEAD_DIM)
    rep = LANES // HEAD_DIM
    return jnp.tile(cos, (1, rep)), jnp.tile(sin, (1, rep))


def _attn_kernel(*refs, n_kv, lam_init, tq):
    lam_ref, sub_ref, q_ref = refs[:3]
    k_refs = refs[3:3 + 2 * n_kv:2]
    v_refs = refs[4:4 + 2 * n_kv:2]
    o_ref, k_scr, vt_scr, st_scr = refs[3 + 2 * n_kv:]
    lk_all = k_scr.shape[0]
    qc = st_scr.shape[2] // 2

    @pl.when(pl.program_id(2) == 0)
    def _():
        off = 0
        for k_ref, v_ref in zip(k_refs, v_refs):
            lk = k_ref.shape[1]
            k_scr[off:off + lk, :] = k_ref[0]
            tc = min(lk, MXU_DIM)
            for c in range(lk // tc):
                vt_scr[:, off + c * tc:off + (c + 1) * tc] = v_ref[0, c * tc:(c + 1) * tc, :].astype(F32).T.astype(BF16)
            off += lk

    lv = lam_ref[...]
    lam = (jnp.exp(jnp.sum(lv[0:1] * lv[1:2], axis=1, keepdims=True))
           - jnp.exp(jnp.sum(lv[2:3] * lv[3:4], axis=1, keepdims=True)) + lam_init)
    lane = lax.broadcasted_iota(jnp.int32, (1, V_HEAD_DIM), 1)

    def stacked_queries(c):
        q = q_ref[0, pl.ds(pl.multiple_of(c * qc, qc), qc), :]
        zero = jnp.zeros_like(q)
        return jnp.concatenate([jnp.where(lane < HEAD_DIM, q, zero), jnp.where(lane >= HEAD_DIM, q, zero)], axis=0)

    def scores(c, slot):
        st = lax.dot_general(k_scr[...], stacked_queries(c), (((1,), (1,)), ((), ())),
                             preferred_element_type=F32)
        st_scr[slot] = st
        return jnp.max(st, axis=0, keepdims=True)

    def chunk(c, slot, m, nxt):
        m_next = scores(c + 1, 1 - slot) if nxt else None
        e = jnp.exp2(st_scr[slot] - m)
        r = 1.0 / jnp.sum(e, axis=0, keepdims=True)
        acc = _dot(vt_scr[...], e.astype(BF16))
        ot = acc[:, :qc] * r[:, :qc] - acc[:, qc:] * (lam * r[:, qc:])
        o = _rms(ot.T, sub_ref[...]) * (1.0 - lam_init)
        o_ref[0, pl.ds(pl.multiple_of(c * qc, qc), qc), :] = o.astype(o_ref.dtype)
        return m_next

    n_chunks = tq // qc
    assert n_chunks == 1 or n_chunks % 2 == 0
    m = scores(0, 0)
    if n_chunks > 1:
        def pair(i, m):
            return chunk(2 * i + 1, 1, chunk(2 * i, 0, m, True), True)
        m = lax.fori_loop(0, n_chunks // 2 - 1, pair, m)
        m = chunk(n_chunks - 2, 0, m, True)
        chunk(n_chunks - 1, 1, m, False)
    else:
        chunk(0, 0, m, False)


def _diff_attention(q_src, kv_srcs, lam_vec, subln, lam_init, d, tq=2048, qc=256):
    bsz, lq, _ = q_src.shape
    n_heads = d // V_HEAD_DIM
    tq = _row_tile(lq, tq)
    qc = _row_tile(tq, qc)
    lk_all = sum(src.shape[1] for src in kv_srcs)
    ins = [lam_vec, subln.reshape(1, V_HEAD_DIM), q_src]
    specs = [pl.BlockSpec(lam_vec.shape, lambda b, h, i: (0, 0)),
             pl.BlockSpec((1, V_HEAD_DIM), lambda b, h, i: (0, 0)),
             pl.BlockSpec((1, tq, V_HEAD_DIM), lambda b, h, i: (b, i, h))]
    for src in kv_srcs:
        lk = src.shape[1]
        ins += [src, src]
        specs += [pl.BlockSpec((1, lk, V_HEAD_DIM), lambda b, h, i: (b, 0, n_heads + h)),
                  pl.BlockSpec((1, lk, V_HEAD_DIM), lambda b, h, i: (b, 0, 2 * n_heads + h))]
    return pl.pallas_call(
        functools.partial(_attn_kernel, n_kv=len(kv_srcs), lam_init=lam_init, tq=tq),
        grid=(bsz, n_heads, lq // tq),
        in_specs=specs,
        out_specs=pl.BlockSpec((1, tq, V_HEAD_DIM), lambda b, h, i: (b, i, h)),
        out_shape=jax.ShapeDtypeStruct((bsz, lq, d), BF16),
        scratch_shapes=[pltpu.VMEM((lk_all, V_HEAD_DIM), BF16), pltpu.VMEM((V_HEAD_DIM, lk_all), BF16),
                        pltpu.VMEM((2, lk_all, 2 * qc), F32)],
        compiler_params=_params("arbitrary", "arbitrary", "arbitrary"),
        name="diff_attention",
    )(*ins)


def _mlp_kernel(*refs, mix, mix_bias):
    refs = list(refs)
    if mix:
        y_ref, wo_ref = refs.pop(0), refs.pop(0)
        bo_ref = refs.pop(0) if mix_bias else None
        gmix_ref = refs.pop(0)
    h_ref, gpre_ref, gpost_ref, mod_ref, w1_ref, w2_ref, o_ref, hid_ref = refs
    x = h_ref[...]
    if mix:
        y = _dot(y_ref[...], wo_ref[...])
        if mix_bias:
            y = y + bo_ref[...]
        x = x + mod_ref[0, 2:3, :] * _rms(y, gmix_ref[...])
    v = (_rms(x, gpre_ref[...]) * (1.0 + mod_ref[0, 4:5, :]) + mod_ref[0, 3:4, :]).astype(BF16)
    dff = w1_ref.shape[1]
    fc = 4 * MXU_DIM
    for c in range(dff // fc):
        hid = jnp.maximum(_dot(v, w1_ref[:, c * fc:(c + 1) * fc]), 0.0)
        hid_ref[:, c * fc:(c + 1) * fc] = (hid * hid).astype(BF16)
    y = _dot(hid_ref[...], w2_ref[...])
    o_ref[...] = x + mod_ref[0, 5:6, :] * _rms(y, gpost_ref[...])


def _mlp_sublayer(h, gpre, gpost, mod, w1, w2, *, rows_per_mod, mix=None, tm=512):
    rows, d = h.shape
    dff = w1.shape[1]
    tm = _row_tile(min(rows, rows_per_mod), tm)
    tile = pl.BlockSpec((tm, d), lambda i: (i, 0))
    vec = pl.BlockSpec((1, d), lambda i: (0, 0))
    ins, specs, mix_bias = [], [], False
    if mix is not None:
        y, wo, bo, gmix = mix
        mix_bias = bo is not None
        ins = [y, wo] + ([bo.reshape(1, d)] if mix_bias else []) + [gmix.reshape(1, d)]
        specs = [tile, _resident((d, d), lambda i: (0, 0))] + ([vec] if mix_bias else []) + [vec]
    return pl.pallas_call(
        functools.partial(_mlp_kernel, mix=mix is not None, mix_bias=mix_bias),
        grid=(rows // tm,),
        in_specs=specs + [tile, vec, vec,
                          pl.BlockSpec((1, N_MOD, d), lambda i: (i * tm // rows_per_mod, 0, 0)),
                          _resident((d, dff), lambda i: (0, 0)),
                          _resident((dff, d), lambda i: (0, 0))],
        out_specs=tile,
        out_shape=jax.ShapeDtypeStruct((rows, d), F32),
        scratch_shapes=[pltpu.VMEM((tm, dff), BF16)],
        compiler_params=_params("arbitrary"),
        name="mlp_sublayer",
    )(*ins, h, gpre.reshape(1, d), gpost.reshape(1, d), mod, w1, w2)


def _hyena_filter_kernel(z_ref, w1_ref, b1_ref, w2_ref, b2_ref, fr_ref, wo_ref, dl_ref, hf_ref, hb_ref):
    hi = lax.Precision.HIGHEST
    z = z_ref[...]
    freq = fr_ref[...]
    hdn = jnp.sin(freq * (jnp.dot(z, w1_ref[...], precision=hi, preferred_element_type=F32) + b1_ref[...]))
    for j in range(w2_ref.shape[0]):
        hdn = jnp.sin(freq * (jnp.dot(hdn, w2_ref[j], precision=hi, preferred_element_type=F32) + b2_ref[j]))
    h = jnp.dot(hdn, wo_ref[...], precision=hi, preferred_element_type=F32)
    d = hf_ref.shape[1]
    window = jnp.exp(-z[:, 0:1] * dl_ref[...])
    hf_ref[...] = h[:, :d] * window
    hb_ref[...] = h[:, d:] * window


def _hyena_filters(n, w1, b1, w2, b2, freq, w_out, d):
    bands = (HYENA_EMB_DIM - 1) // 2
    t = jnp.linspace(0.0, 1.0, n, dtype=F32)[:, None]
    w = 2.0 * math.pi * jnp.arange(n, dtype=F32)[:, None] / n
    f = jnp.linspace(1e-4, bands - 1, bands, dtype=F32)[None, :]
    z = jnp.concatenate([t, jnp.cos(f * w), -jnp.sin(f * w),
                         jnp.zeros((n, LANES - HYENA_EMB_DIM), F32)], axis=-1)
    w1p = jnp.concatenate([w1, jnp.zeros((LANES - HYENA_EMB_DIM, w1.shape[1]), F32)], axis=0)
    deltas = jnp.abs(jnp.linspace(math.log(HYENA_DECAY_TARGET) / HYENA_SLOW_DECAY_PCT,
                                  math.log(HYENA_DECAY_TARGET) / HYENA_FAST_DECAY_PCT, d, dtype=F32))
    order = w1.shape[1]
    tr = _row_tile(n, 256)
    full = lambda *shape: pl.BlockSpec(shape, lambda i: (0,) * len(shape))
    return pl.pallas_call(
        _hyena_filter_kernel,
        grid=(n // tr,),
        in_specs=[pl.BlockSpec((tr, LANES), lambda i: (i, 0)),
                  full(LANES, order), full(1, order), full(*w2.shape), full(w2.shape[0], 1, order),
                  full(1, order), full(order, 2 * d), full(1, d)],
        out_specs=[pl.BlockSpec((tr, d), lambda i: (i, 0))] * 2,
        out_shape=[jax.ShapeDtypeStruct((n, d), F32)] * 2,
        compiler_params=_params("arbitrary"),
        name="hyena_filters",
    )(z, w1p, b1.reshape(1, order), w2, b2.reshape(w2.shape[0], 1, order), freq.reshape(1, order),
      w_out, deltas.reshape(1, d))


def _dft_matrices(n):
    k = jnp.arange(n, dtype=jnp.int32)
    blk = min(n, LANES)

    def tables(t):
        ang = ((k[:, None] * t[None, :]) % (2 * n)).astype(F32) * (math.pi / n)
        return jnp.cos(ang), jnp.sin(ang)

    ca, sa = tables(jnp.arange(n // blk, dtype=jnp.int32) * blk)
    cb, sb = tables(jnp.arange(blk, dtype=jnp.int32))
    cmat = ca[:, :, None] * cb[:, None, :] - sa[:, :, None] * sb[:, None, :]
    smat = sa[:, :, None] * cb[:, None, :] + ca[:, :, None] * sb[:, None, :]
    return cmat.reshape(n, n).astype(BF16), smat.reshape(n, n).astype(BF16)


def _alt_sign(n):
    row = lax.broadcasted_iota(jnp.int32, (n, 1), 0)
    return jnp.where(row % 2 == 0, 1.0, -1.0).astype(F32), row


def _hyena_spec_kernel(c_ref, s_ref, hf_ref, hb_ref, kr_ref, ks_ref, kn_ref):
    n = hf_ref.shape[0]
    sgn, row = _alt_sign(n)
    hf = hf_ref[...]
    hb = jnp.where(row == 0, 0.0, hb_ref[...])
    wk = jnp.where(row == 0, 0.5 / n, 1.0 / n)
    kr_ref[...] = _dot(c_ref[...], (hf + hb).astype(BF16)) * wk
    ks_ref[...] = _dot(s_ref[...], (hf - hb).astype(BF16)) * wk
    kn_ref[...] = jnp.sum(sgn * (hf + hb), axis=0, keepdims=True) * (0.5 / n)


def _hyena_spectrum(cmat, smat, hf, hb):
    n, d = hf.shape
    assert n % 2 == 0
    cb = min(d, MXU_DIM)
    return pl.pallas_call(
        _hyena_spec_kernel,
        grid=(d // cb,),
        in_specs=[_resident((n, n), lambda j: (0, 0)), _resident((n, n), lambda j: (0, 0)),
                  pl.BlockSpec((n, cb), lambda j: (0, j)), pl.BlockSpec((n, cb), lambda j: (0, j))],
        out_specs=[pl.BlockSpec((n, cb), lambda j: (0, j)), pl.BlockSpec((n, cb), lambda j: (0, j)),
                   pl.BlockSpec((1, cb), lambda j: (0, j))],
        out_shape=[jax.ShapeDtypeStruct((n, d), F32), jax.ShapeDtypeStruct((n, d), F32),
                   jax.ShapeDtypeStruct((1, d), F32)],
        compiler_params=_params("arbitrary"),
        name="hyena_spectrum",
    )(cmat, smat, hf, hb)


def _hyena_conv_kernel(zx0_ref, zx1_ref, zv_ref, w0_ref, w1_ref, wv_ref, b0_ref, b1_ref, bv_ref,
                       c_ref, s_ref, kr_ref, ks_ref, kn_ref, db_ref, o_ref,
                       x0_scr, vv_scr, vb_scr, p_scr, q_scr):
    n = o_ref.shape[1]
    sgn, row = _alt_sign(n)

    def short_conv(z_ref, w_ref, b_ref):
        x = z_ref[0].astype(F32)
        prev = jnp.where(row == 0, 0.0, pltpu.roll(x, 1, axis=0))
        nxt = jnp.where(row == n - 1, 0.0, pltpu.roll(x, n - 1, axis=0))
        return w_ref[0:1, :] * prev + w_ref[1:2, :] * x + w_ref[2:3, :] * nxt + b_ref[...]

    x0_scr[...] = short_conv(zx0_ref, w0_ref, b0_ref)
    vv = short_conv(zv_ref, wv_ref, bv_ref) * short_conv(zx1_ref, w1_ref, b1_ref)
    vv_scr[...] = vv
    vb_scr[...] = vv.astype(BF16)
    nyq = jnp.sum(vv * sgn, axis=0, keepdims=True) * kn_ref[...]

    ch = min(n, 2 * MXU_DIM)
    for f in range(n // ch):
        sl = slice(f * ch, (f + 1) * ch)
        fr = _dot(c_ref[sl, :], vb_scr[...])
        fs = _dot(s_ref[sl, :], vb_scr[...])
        kr, ks = kr_ref[sl, :], ks_ref[sl, :]
        p_scr[sl, :] = (fr * kr - fs * ks).astype(BF16)
        q_scr[sl, :] = (fr * ks + fs * kr).astype(BF16)
    for t in range(n // ch):
        sl = slice(t * ch, (t + 1) * ch)
        y = _dot(c_ref[sl, :], p_scr[...]) + _dot(s_ref[sl, :], q_scr[...])
        y = y + sgn[sl, :] * nyq + vv_scr[sl, :] * db_ref[...]
        o_ref[0, sl, :] = (y * x0_scr[sl, :]).astype(o_ref.dtype)


def _hyena_conv(z, w_short, b_short, cmat, smat, kr, ks, kn, d_bias):
    bsz, n, d3 = z.shape
    d = d3 // 3
    cb = min(d, MXU_DIM)
    nb = d // cb
    zspec = lambda g: pl.BlockSpec((1, n, cb), lambda j, b: (b, 0, g * nb + j))
    wspec = lambda g: pl.BlockSpec((w_short.shape[0], cb), lambda j, b: (0, g * nb + j))
    bspec = lambda g: pl.BlockSpec((1, cb), lambda j, b: (0, g * nb + j))
    col = pl.BlockSpec((n, cb), lambda j, b: (0, j))
    vec = pl.BlockSpec((1, cb), lambda j, b: (0, j))
    bs2 = b_short.reshape(1, d3)
    return pl.pallas_call(
        _hyena_conv_kernel,
        grid=(nb, bsz),
        in_specs=[zspec(0), zspec(1), zspec(2), wspec(0), wspec(1), wspec(2), bspec(0), bspec(1), bspec(2),
                  _resident((n, n), lambda j, b: (0, 0)), _resident((n, n), lambda j, b: (0, 0)),
                  col, col, vec, vec],
        out_specs=pl.BlockSpec((1, n, cb), lambda j, b: (b, 0, j)),
        out_shape=jax.ShapeDtypeStruct((bsz, n, d), BF16),
        scratch_shapes=[pltpu.VMEM((n, cb), F32), pltpu.VMEM((n, cb), F32), pltpu.VMEM((n, cb), BF16),
                        pltpu.VMEM((n, cb), BF16), pltpu.VMEM((n, cb), BF16)],
        compiler_params=_params("arbitrary", "arbitrary"),
        name="hyena_conv",
    )(z, z, z, w_short, w_short, w_short, bs2, bs2, bs2, cmat, smat, kr, ks, kn, d_bias.reshape(1, d))


def _window_dft(nw):
    half = nw // 2
    t = jnp.arange(nw, dtype=jnp.int32)
    ang = ((jnp.arange(half, dtype=jnp.int32)[:, None] * t[None, :]) % nw).astype(F32) * (2.0 * math.pi / nw)
    nyq = jnp.where(t % 2 == 0, 1.0, -1.0).astype(F32)[None, :]
    return jnp.concatenate([jnp.cos(ang), nyq, jnp.sin(ang)[1:]], axis=0)


def _tap_spectrum_kernel(tr_ref, ts_ref, td_ref, w_ref, g_ref):
    hi = lax.Precision.HIGHEST
    for i, t_ref in enumerate((tr_ref, ts_ref, td_ref)):
        g_ref[i] = jnp.dot(t_ref[...], w_ref[...], precision=hi, preferred_element_type=F32)


def _tap_spectrum(w_dw, nw):
    width, d = w_dw.shape
    reach, half = width // 2, nw // 2
    kpad = -width % SUBLANES
    lag = reach - jnp.arange(width + kpad, dtype=jnp.int32)
    ang = ((jnp.arange(half, dtype=jnp.int32)[:, None] * lag[None, :]) % nw).astype(F32) * (2.0 * math.pi / nw)
    wf = jnp.where(jnp.arange(half) == 0, 1.0 / nw, 2.0 / nw).astype(F32)[:, None]
    tr = jnp.cos(ang) * wf
    ts = jnp.sin(ang) * wf
    nyq = jnp.where(lag % 2 == 0, 1.0 / nw, -1.0 / nw).astype(F32)[None, :]
    td = jnp.concatenate([nyq, tr[1:]], axis=0)
    wp = jnp.concatenate([w_dw, jnp.zeros((kpad, d), F32)], axis=0)
    full = lambda a: pl.BlockSpec(a.shape, lambda: (0,) * a.ndim)
    return pl.pallas_call(
        _tap_spectrum_kernel,
        in_specs=[full(tr), full(ts), full(td), full(wp)],
        out_specs=pl.BlockSpec((3, half, d), lambda: (0, 0, 0)),
        out_shape=jax.ShapeDtypeStruct((3, half, d), F32),
        name="tap_spectrum",
    )(tr, ts, td, wp)


def _conformer_post_kernel(x_ref, a_ref, ai_ref, gt_ref, bd_ref, lg_ref, lb_ref, w_ref, b_ref, h_ref, g_ref,
                           mod_ref, o_ref, xp_scr, ys_scr):
    n, d = x_ref.shape[1], x_ref.shape[2]
    nw = a_ref.shape[0]
    half = nw // 2
    tm = o_ref.shape[0]
    j = pl.program_id(1)

    @pl.when(j == 0)
    def _():
        xp_scr[0:CONV_HALO, :] = jnp.zeros((CONV_HALO, d), BF16)
        xp_scr[CONV_HALO:CONV_HALO + n, :] = x_ref[0]
        xp_scr[CONV_HALO + n:, :] = jnp.zeros((CONV_HALO, d), BF16)

    base = pl.multiple_of(j * tm, tm)
    f = _dot(a_ref[...], xp_scr[pl.ds(base, nw), :])
    xr, xs = f[:half], f[half:]
    ys_scr[0:half, :] = (xr * gt_ref[0] - xs * gt_ref[1]).astype(BF16)
    ys_scr[half:, :] = (xr * gt_ref[1] + xs * gt_ref[2]).astype(BF16)
    z = _dot(ai_ref[...], ys_scr[...]) + bd_ref[...]
    mu = jnp.mean(z, axis=-1, keepdims=True)
    zc = z - mu
    var = jnp.mean(zc * zc, axis=-1, keepdims=True)
    zn = zc * lax.rsqrt(var + LN_EPS) * lg_ref[...] + lb_ref[...]
    act = (zn / (1.0 + jnp.exp(-zn))).astype(BF16)
    y = _dot(act, w_ref[...]) + b_ref[...]
    o_ref[...] = h_ref[...] + mod_ref[0, 2:3, :] * _rms(y, g_ref[...])


def _conformer_post(x, w_dw, b_dw, ln_g, ln_b, w, b, h, g, mod, *, per_batch_mod, tm=256):
    bsz, n, d = x.shape
    assert w_dw.shape[0] // 2 < CONV_HALO
    tm = _row_tile(n, tm)
    per = n // tm
    nw = tm + 2 * CONV_HALO
    amat = _window_dft(nw)
    gtab = _tap_spectrum(w_dw, nw)
    vec = pl.BlockSpec((1, d), lambda bi, j: (0, 0))
    return pl.pallas_call(
        _conformer_post_kernel,
        grid=(bsz, per),
        in_specs=[pl.BlockSpec((1, n, d), lambda bi, j: (bi, 0, 0)),
                  _resident((nw, nw), lambda bi, j: (0, 0)), _resident((tm, nw), lambda bi, j: (0, 0)),
                  _resident((3, nw // 2, d), lambda bi, j: (0, 0, 0)), vec, vec, vec,
                  _resident((d, d), lambda bi, j: (0, 0)), vec,
                  pl.BlockSpec((tm, d), lambda bi, j: (bi * per + j, 0)), vec,
                  pl.BlockSpec((1, N_MOD, d), lambda bi, j: (bi if per_batch_mod else 0, 0, 0))],
        out_specs=pl.BlockSpec((tm, d), lambda bi, j: (bi * per + j, 0)),
        out_shape=jax.ShapeDtypeStruct((bsz * n, d), F32),
        scratch_shapes=[pltpu.VMEM((n + 2 * CONV_HALO, d), BF16), pltpu.VMEM((nw, d), BF16)],
        compiler_params=_params("arbitrary", "arbitrary"),
        name="conformer_post",
    )(x, amat.astype(BF16), amat.T[CONV_HALO:CONV_HALO + tm].astype(BF16), gtab, b_dw.reshape(1, d),
      ln_g.reshape(1, d), ln_b.reshape(1, d), w, b.reshape(1, d), h, g.reshape(1, d), mod)


def kernel(x, c, ctx, c_ctx, w_mod, b_mod, norm_mix_pre, norm_mix_post, norm_mlp_pre, norm_mlp_post, w_mlp_in, w_mlp_out, attn_w_qkv, attn_w_out, attn_lambda, attn_subln, hy_w_in, hy_b_in, hy_w_short, hy_b_short, hy_filt_w1, hy_filt_b1, hy_filt_w2, hy_filt_b2, hy_filt_freq, hy_filt_w_out, hy_bias, hy_w_out, hy_b_out, cv_w_pw1, cv_b_pw1, cv_w_dw, cv_b_dw, cv_ln_g, cv_ln_b, cv_w_pw2, cv_b_pw2):
    bsz, seq, d = x.shape
    n_ctx = ctx.shape[1]
    depth = w_mod.shape[0]
    assert bsz + 1 <= MOD_ROWS and d % (2 * MXU_DIM) == 0

    cc = jnp.concatenate([c, c_ctx[None, :], jnp.zeros((MOD_ROWS - bsz - 1, d), F32)], axis=0)
    mods = _mod_vectors(cc, w_mod, b_mod)
    rope_tabs = _rope_tables(seq)

    h_lat = x.reshape(bsz * seq, d)
    h_ctx = ctx.reshape(bsz * n_ctx, d)
    lat = dict(rows_per_mod=seq)
    cx = dict(rows_per_mod=bsz * n_ctx)
    for i in range(depth):
        last = i == depth - 1
        kind, j = i % N_MIXERS, i // N_MIXERS
        ctx_out = not last
        mod_l = mods[i, :bsz].reshape(bsz, N_MOD, d)
        mod_c = mods[i, bsz:bsz + 1].reshape(1, N_MOD, d)
        g_pre, g_post = norm_mix_pre[i], norm_mix_post[i]
        mix_l = mix_c = None
        if kind == 0:
            lam_init = 0.8 - 0.6 * math.exp(-0.3 * i)
            w_qkv, w_o = attn_w_qkv[j].astype(BF16), attn_w_out[j].astype(BF16)
            z_l = _pre_project(h_lat, g_pre, mod_l, w_qkv, None, kind="attn", rope_tabs=rope_tabs, **lat)
            z_c = _pre_project(h_ctx, g_pre, mod_c, w_qkv, None, kind="attn", **cx)
            z_l = z_l.reshape(bsz, seq, 3 * d)
            z_c = z_c.reshape(bsz, n_ctx, 3 * d)
            o_l = _diff_attention(z_l, [z_c, z_l], attn_lambda[j], attn_subln[j], lam_init, d)
            mix_l = (o_l.reshape(bsz * seq, d), w_o, None, g_post)
            if ctx_out:
                o_c = _diff_attention(z_c, [z_c], attn_lambda[j], attn_subln[j], lam_init, d)
                mix_c = (o_c.reshape(bsz * n_ctx, d), w_o, None, g_post)
        elif kind == 1:
            w_in, w_o = hy_w_in[j].astype(BF16), hy_w_out[j].astype(BF16)

            def hyena(h, mod, n, rpm):
                z = _pre_project(h, g_pre, mod, w_in, hy_b_in[j], kind="hyena", rows_per_mod=rpm)
                hf, hb = _hyena_filters(n, hy_filt_w1[j], hy_filt_b1[j], hy_filt_w2[j], hy_filt_b2[j],
                                        hy_filt_freq[j], hy_filt_w_out[j], d)
                cmat, smat = _dft_matrices(n)
                kr, ks, kn = _hyena_spectrum(cmat, smat, hf, hb)
                g = _hyena_conv(z.reshape(bsz, n, 3 * d), hy_w_short[j], hy_b_short[j], cmat, smat,
                                kr, ks, kn, hy_bias[j])
                return (g.reshape(bsz * n, d), w_o, hy_b_out[j], g_post)

            mix_l = hyena(h_lat, mod_l, seq, seq)
            if ctx_out:
                mix_c = hyena(h_ctx, mod_c, n_ctx, bsz * n_ctx)
        else:
            w_1, w_2 = cv_w_pw1[j].astype(BF16), cv_w_pw2[j].astype(BF16)

            def conformer(h, mod, n, rpm, per_batch_mod):
                a = _pre_project(h, g_pre, mod, w_1, cv_b_pw1[j], kind="conformer", rows_per_mod=rpm)
                return _conformer_post(a.reshape(bsz, n, d), cv_w_dw[j], cv_b_dw[j], cv_ln_g[j], cv_ln_b[j],
                                       w_2, cv_b_pw2[j], h, g_post, mod, per_batch_mod=per_batch_mod)

            h_lat = conformer(h_lat, mod_l, seq, seq, True)
            if ctx_out:
                h_ctx = conformer(h_ctx, mod_c, n_ctx, bsz * n_ctx, False)
        w1, w2 = w_mlp_in[i].astype(BF16), w_mlp_out[i].astype(BF16)
        h_lat = _mlp_sublayer(h_lat, norm_mlp_pre[i], norm_mlp_post[i], mod_l, w1, w2, mix=mix_l, **lat)
        if ctx_out:
            h_ctx = _mlp_sublayer(h_ctx, norm_mlp_pre[i], norm_mlp_post[i], mod_c, w1, w2, mix=mix_c, **cx)
    return h_lat.reshape(bsz, seq, d)
```

```python
import functools
import math

import jax
import jax.numpy as jnp
from jax import lax
from jax.experimental import pallas as pl
from jax.experimental.pallas import tpu as pltpu

F32 = jnp.float32
BF16 = jnp.bfloat16

N_MIXERS = 3
N_MOD = 6
HEAD_DIM = 64
V_HEAD_DIM = 2 * HEAD_DIM
GRID_W = 64
ROPE_THETA = 10000.0
NORM_EPS = 1e-6
LN_EPS = 1e-5
HYENA_EMB_DIM = 33
HYENA_DECAY_TARGET = 1e-2
HYENA_FAST_DECAY_PCT = 0.3
HYENA_SLOW_DECAY_PCT = 1.5

LANES = 128
SUBLANES = 8
MXU_DIM = 256
VMEM_LIMIT_BYTES = 56 * 1024 * 1024
MOD_ROWS = 24
CONV_HALO = 16


def _params(*sem):
    return pltpu.CompilerParams(dimension_semantics=sem, vmem_limit_bytes=VMEM_LIMIT_BYTES)


def _resident(shape, index_map):
    return pl.BlockSpec(shape, index_map, pipeline_mode=pl.Buffered(1))


def _row_tile(rows, want):
    t = min(rows, want)
    assert rows % t == 0
    return t


def _rms(x, g):
    return x * lax.rsqrt(jnp.mean(x * x, axis=-1, keepdims=True) + NORM_EPS) * g


def _dot(a, b):
    return jnp.dot(a, b, preferred_element_type=F32)


def _mod_kernel(c_ref, w_ref, b_ref, o_ref):
    c = c_ref[...]
    s = (c / (1.0 + jnp.exp(-c))).astype(BF16)
    o_ref[0] = _dot(s, w_ref[0].astype(BF16)) + b_ref[0]


def _mod_vectors(cc, w_mod, b_mod):
    depth, d, n = w_mod.shape
    tn = 1024
    return pl.pallas_call(
        _mod_kernel,
        grid=(depth, n // tn),
        in_specs=[pl.BlockSpec((MOD_ROWS, d), lambda i, j: (0, 0)),
                  pl.BlockSpec((1, d, tn), lambda i, j: (i, 0, j)),
                  pl.BlockSpec((1, 1, tn), lambda i, j: (i, 0, j))],
        out_specs=pl.BlockSpec((1, MOD_ROWS, tn), lambda i, j: (i, 0, j)),
        out_shape=jax.ShapeDtypeStruct((depth, MOD_ROWS, n), F32),
        compiler_params=_params("arbitrary", "arbitrary"),
        name="mod_vectors",
    )(cc, w_mod, b_mod.reshape(depth, 1, n))


def _rope_group(z, cos, sin, first_half):
    partner = jnp.where(first_half, pltpu.roll(z, LANES - 16, axis=1), pltpu.roll(z, 16, axis=1))
    return z * cos + partner * sin


def _pre_kernel(*refs, kind, d, rope):
    h_ref, g_ref, mod_ref, w_ref = refs[:4]
    rest = list(refs[4:])
    b_ref = rest.pop(0) if kind != "attn" else None
    cos_ref, sin_ref = (rest.pop(0), rest.pop(0)) if rope else (None, None)
    if kind == "attn":
        oq_ref, ok_ref, ovt_ref = rest
    else:
        (o_ref,) = rest

    u = _rms(h_ref[...], g_ref[...]) * (1.0 + mod_ref[0, 1:2, :]) + mod_ref[0, 0:1, :]
    ub = u.astype(BF16)
    nc = 2 * MXU_DIM
    if kind == "conformer":
        for c in range(d // nc):
            a = _dot(ub, w_ref[:, c * nc:(c + 1) * nc]) + b_ref[:, c * nc:(c + 1) * nc]
            g = _dot(ub, w_ref[:, d + c * nc:d + (c + 1) * nc]) + b_ref[:, d + c * nc:d + (c + 1) * nc]
            o_ref[:, c * nc:(c + 1) * nc] = (a / (1.0 + jnp.exp(-g))).astype(o_ref.dtype)
        return
    if rope:
        lane = lax.broadcasted_iota(jnp.int32, (1, LANES), 1)
        first_half = (lane % 32) < 16
        cos, sin = cos_ref[...], sin_ref[...]
    for c in range(3 * d // nc):
        z = _dot(ub, w_ref[:, c * nc:(c + 1) * nc])
        if kind == "hyena":
            z = z + b_ref[:, c * nc:(c + 1) * nc]
            o_ref[:, c * nc:(c + 1) * nc] = z.astype(o_ref.dtype)
            continue
        section = c * nc // d
        for gi in range(nc // LANES):
            head = (c * nc - section * d) // LANES + gi
            zg = z[:, gi * LANES:(gi + 1) * LANES]
            if rope and section < 2:
                zg = _rope_group(zg, cos, sin, first_half)
            if section == 0:
                oq_ref[0, head] = (zg * (HEAD_DIM ** -0.5 * math.log2(math.e))).astype(oq_ref.dtype)
            elif section == 1:
                ok_ref[0, head] = zg.astype(ok_ref.dtype)
            else:
                ovt_ref[0, head] = zg.T.astype(ovt_ref.dtype)


def _pre_project(h, g, mod, w, b, *, kind, rows_per_mod, seq=None, rope_tabs=None, tm=512):
    rows, d = h.shape
    n_in = w.shape[1]
    n_out = d if kind == "conformer" else n_in
    tm = _row_tile(min(rows, rows_per_mod, seq or rows), tm)
    rope = rope_tabs is not None
    if kind == "attn":
        per, nh = seq // tm, d // V_HEAD_DIM
        qk_spec = pl.BlockSpec((1, nh, tm, V_HEAD_DIM), lambda i: (i // per, 0, i % per, 0))
        out_specs = [qk_spec, qk_spec, pl.BlockSpec((1, nh, V_HEAD_DIM, tm), lambda i: (i // per, 0, 0, i % per))]
        qk_shape = jax.ShapeDtypeStruct((rows // seq, nh, seq, V_HEAD_DIM), BF16)
        out_shape = [qk_shape, qk_shape, jax.ShapeDtypeStruct((rows // seq, nh, V_HEAD_DIM, seq), BF16)]
    else:
        out_specs = pl.BlockSpec((tm, n_out), lambda i: (i, 0))
        out_shape = jax.ShapeDtypeStruct((rows, n_out), BF16)
    ins = [h, g.reshape(1, d), mod, w]
    specs = [pl.BlockSpec((tm, d), lambda i: (i, 0)),
             pl.BlockSpec((1, d), lambda i: (0, 0)),
             pl.BlockSpec((1, N_MOD, d), lambda i: (i * tm // rows_per_mod, 0, 0)),
             _resident((d, n_in), lambda i: (0, 0))]
    if kind != "attn":
        ins.append(b.reshape(1, n_in))
        specs.append(pl.BlockSpec((1, n_in), lambda i: (0, 0)))
    if rope:
        assert rope_tabs[0].shape[0] == seq
        ins += list(rope_tabs)
        specs += [pl.BlockSpec((tm, LANES), lambda i: (i % per, 0))] * 2
    return pl.pallas_call(
        functools.partial(_pre_kernel, kind=kind, d=d, rope=rope),
        grid=(rows // tm,),
        in_specs=specs,
        out_specs=out_specs,
        out_shape=out_shape,
        compiler_params=_params("arbitrary"),
        name=f"pre_{kind}",
    )(*ins)


def _rope_tables(seq):
    t = jnp.arange(seq)
    pos = jnp.stack([(t // GRID_W).astype(F32), (t % GRID_W).astype(F32)], axis=1)
    half = HEAD_DIM // 4
    inv_freq = ROPE_THETA ** (-jnp.arange(half, dtype=F32) / half)
    ang = pos[:, :, None] * inv_freq[None, None, :]
    cos = jnp.concatenate([jnp.cos(ang), jnp.cos(ang)], axis=-1).reshape(seq, HEAD_DIM)
    sin = jnp.concatenate([-jnp.sin(ang), jnp.sin(ang)], axis=-1).reshape(seq, HEAD_DIM)
    rep = LANES // HEAD_DIM
    return jnp.tile(cos, (1, rep)), jnp.tile(sin, (1, rep))


def _attn_kernel(*refs, n_kv, lam_init, tq):
    lam_ref, sub_ref, q_ref = refs[:3]
    k_refs = refs[3:3 + 2 * n_kv:2]
    vt_refs = refs[4:4 + 2 * n_kv:2]
    o_ref, st_scr, acc_scr, den_scr = refs[3 + 2 * n_kv:]
    lk_all = st_scr.shape[1]
    qc = st_scr.shape[2] // 2
    kp = min(MXU_DIM, *(k_ref.shape[2] for k_ref in k_refs))
    parts = [(k_ref, vt_ref, o) for k_ref, vt_ref in zip(k_refs, vt_refs) for o in range(0, k_ref.shape[2], kp)]
    n_parts = len(parts)
    assert n_parts * kp == lk_all

    lv = lam_ref[...]
    lam = (jnp.exp(jnp.sum(lv[0:1] * lv[1:2], axis=1, keepdims=True))
           - jnp.exp(jnp.sum(lv[2:3] * lv[3:4], axis=1, keepdims=True)) + lam_init)
    lane = lax.broadcasted_iota(jnp.int32, (1, V_HEAD_DIM), 1)

    def stacked_queries(c):
        q = q_ref[0, 0, pl.ds(pl.multiple_of(c * qc, qc), qc), :]
        zero = jnp.zeros_like(q)
        return jnp.concatenate([jnp.where(lane < HEAD_DIM, q, zero), jnp.where(lane >= HEAD_DIM, q, zero)], axis=0)

    def scores_part(qq, slot, p):
        k_ref, _, off = parts[p]
        st = lax.dot_general(k_ref[0, 0, off:off + kp, :], qq, (((1,), (1,)), ((), ())),
                             preferred_element_type=F32)
        st_scr[slot, p * kp:(p + 1) * kp, :] = st
        return jnp.max(st, axis=0, keepdims=True)

    def scores(c, slot):
        qq = stacked_queries(c)
        return functools.reduce(jnp.maximum, [scores_part(qq, slot, p) for p in range(n_parts)])

    def finish(c, slot):
        acc = acc_scr[slot]
        r = 1.0 / den_scr[slot]
        ot = acc[:, :qc] * r[:, :qc] - acc[:, qc:] * (lam * r[:, qc:])
        o = _rms(ot.T, sub_ref[...]) * (1.0 - lam_init)
        o_ref[0, 0, pl.ds(pl.multiple_of(c * qc, qc), qc), :] = o.astype(o_ref.dtype)

    def chunk(c, slot, m, nxt, prev):
        qq_next = stacked_queries(c + 1) if nxt else None
        m_next = denom = acc = None
        for p in range(n_parts):
            if nxt:
                pm = scores_part(qq_next, 1 - slot, p)
                m_next = pm if m_next is None else jnp.maximum(m_next, pm)
            if prev and p == 0:
                finish(c - 1, 1 - slot)
            e = jnp.exp2(st_scr[slot, p * kp:(p + 1) * kp, :] - m)
            ds = jnp.sum(e, axis=0, keepdims=True)
            _, vt_ref, off = parts[p]
            da = _dot(vt_ref[0, 0, :, off:off + kp], e.astype(BF16))
            denom = ds if denom is None else denom + ds
            acc = da if acc is None else acc + da
        acc_scr[slot] = acc
        den_scr[slot] = denom
        return m_next

    n_chunks = tq // qc
    assert n_chunks == 1 or n_chunks % 2 == 0
    m = scores(0, 0)
    if n_chunks > 1:
        m = chunk(0, 0, m, True, False)

        def pair(i, m):
            return chunk(2 * i + 2, 0, chunk(2 * i + 1, 1, m, True, True), True, True)
        m = lax.fori_loop(0, n_chunks // 2 - 1, pair, m)
        chunk(n_chunks - 1, 1, m, False, True)
        finish(n_chunks - 1, 1)
    else:
        chunk(0, 0, m, False, False)
        finish(0, 0)


def _diff_attention(q, kv_srcs, lam_vec, subln, lam_init, tq=2048, qc=256):
    bsz, n_heads, lq, _ = q.shape
    tq = _row_tile(lq, tq)
    qc = _row_tile(tq, qc)
    lk_all = sum(k.shape[2] for k, _ in kv_srcs)
    ins = [lam_vec, subln.reshape(1, V_HEAD_DIM), q]
    specs = [pl.BlockSpec(lam_vec.shape, lambda b, h, i: (0, 0)),
             pl.BlockSpec((1, V_HEAD_DIM), lambda b, h, i: (0, 0)),
             pl.BlockSpec((1, 1, tq, V_HEAD_DIM), lambda b, h, i: (b, h, i, 0))]
    for k, vt in kv_srcs:
        lk = k.shape[2]
        ins += [k, vt]
        specs += [pl.BlockSpec((1, 1, lk, V_HEAD_DIM), lambda b, h, i: (b, h, 0, 0)),
                  pl.BlockSpec((1, 1, V_HEAD_DIM, lk), lambda b, h, i: (b, h, 0, 0))]
    return pl.pallas_call(
        functools.partial(_attn_kernel, n_kv=len(kv_srcs), lam_init=lam_init, tq=tq),
        grid=(bsz, n_heads, lq // tq),
        in_specs=specs,
        out_specs=pl.BlockSpec((1, 1, tq, V_HEAD_DIM), lambda b, h, i: (b, h, i, 0)),
        out_shape=jax.ShapeDtypeStruct((bsz, n_heads, lq, V_HEAD_DIM), BF16),
        scratch_shapes=[pltpu.VMEM((2, lk_all, 2 * qc), F32), pltpu.VMEM((2, V_HEAD_DIM, 2 * qc), F32),
                        pltpu.VMEM((2, 1, 2 * qc), F32)],
        compiler_params=_params("arbitrary", "arbitrary", "arbitrary"),
        name="diff_attention",
    )(*ins)


def _mlp_kernel(*refs, mix, mix_bias):
    refs = list(refs)
    if mix:
        y_ref, wo_ref = refs.pop(0), refs.pop(0)
        bo_ref = refs.pop(0) if mix_bias else None
        gmix_ref = refs.pop(0)
    h_ref, gpre_ref, gpost_ref, mod_ref, w1_ref, w2_ref, o_ref, hid_ref = refs
    x = h_ref[...]
    if mix:
        if len(y_ref.shape) == 4:
            yin = jnp.concatenate([y_ref[0, hd] for hd in range(y_ref.shape[1])], axis=1)
        else:
            yin = y_ref[...]
        y = _dot(yin, wo_ref[...])
        if mix_bias:
            y = y + bo_ref[...]
        x = x + mod_ref[0, 2:3, :] * _rms(y, gmix_ref[...])
    v = (_rms(x, gpre_ref[...]) * (1.0 + mod_ref[0, 4:5, :]) + mod_ref[0, 3:4, :]).astype(BF16)
    dff = w1_ref.shape[1]
    fc = 4 * MXU_DIM
    for c in range(dff // fc):
        hid = jnp.maximum(_dot(v, w1_ref[:, c * fc:(c + 1) * fc]), 0.0)
        hid_ref[:, c * fc:(c + 1) * fc] = (hid * hid).astype(BF16)
    y = _dot(hid_ref[...], w2_ref[...])
    o_ref[...] = x + mod_ref[0, 5:6, :] * _rms(y, gpost_ref[...])


def _mlp_sublayer(h, gpre, gpost, mod, w1, w2, *, rows_per_mod, mix=None, tm=512):
    rows, d = h.shape
    dff = w1.shape[1]
    head_major = mix is not None and mix[0].ndim == 4
    tm = _row_tile(min(rows, rows_per_mod, mix[0].shape[2] if head_major else rows), tm)
    tile = pl.BlockSpec((tm, d), lambda i: (i, 0))
    vec = pl.BlockSpec((1, d), lambda i: (0, 0))
    ins, specs, mix_bias = [], [], False
    if mix is not None:
        y, wo, bo, gmix = mix
        mix_bias = bo is not None
        ins = [y, wo] + ([bo.reshape(1, d)] if mix_bias else []) + [gmix.reshape(1, d)]
        if head_major:
            per = y.shape[2] // tm
            y_spec = pl.BlockSpec((1, y.shape[1], tm, V_HEAD_DIM), lambda i: (i // per, 0, i % per, 0))
        else:
            y_spec = tile
        specs = [y_spec, _resident((d, d), lambda i: (0, 0))] + ([vec] if mix_bias else []) + [vec]
    return pl.pallas_call(
        functools.partial(_mlp_kernel, mix=mix is not None, mix_bias=mix_bias),
        grid=(rows // tm,),
        in_specs=specs + [tile, vec, vec,
                          pl.BlockSpec((1, N_MOD, d), lambda i: (i * tm // rows_per_mod, 0, 0)),
                          _resident((d, dff), lambda i: (0, 0)),
                          _resident((dff, d), lambda i: (0, 0))],
        out_specs=tile,
        out_shape=jax.ShapeDtypeStruct((rows, d), F32),
        scratch_shapes=[pltpu.VMEM((tm, dff), BF16)],
        compiler_params=_params("arbitrary"),
        name="mlp_sublayer",
    )(*ins, h, gpre.reshape(1, d), gpost.reshape(1, d), mod, w1, w2)


def _hyena_filter_kernel(z_ref, w1_ref, b1_ref, w2_ref, b2_ref, fr_ref, wo_ref, dl_ref, hf_ref, hb_ref):
    hi = lax.Precision.HIGHEST
    z = z_ref[...]
    freq = fr_ref[...]
    hdn = jnp.sin(freq * (jnp.dot(z, w1_ref[...], precision=hi, preferred_element_type=F32) + b1_ref[...]))
    for j in range(w2_ref.shape[0]):
        hdn = jnp.sin(freq * (jnp.dot(hdn, w2_ref[j], precision=hi, preferred_element_type=F32) + b2_ref[j]))
    h = jnp.dot(hdn, wo_ref[...], precision=hi, preferred_element_type=F32)
    d = hf_ref.shape[1]
    window = jnp.exp(-z[:, 0:1] * dl_ref[...])
    hf_ref[...] = h[:, :d] * window
    hb_ref[...] = h[:, d:] * window


def _hyena_filters(n, w1, b1, w2, b2, freq, w_out, d):
    bands = (HYENA_EMB_DIM - 1) // 2
    t = jnp.linspace(0.0, 1.0, n, dtype=F32)[:, None]
    w = 2.0 * math.pi * jnp.arange(n, dtype=F32)[:, None] / n
    f = jnp.linspace(1e-4, bands - 1, bands, dtype=F32)[None, :]
    z = jnp.concatenate([t, jnp.cos(f * w), -jnp.sin(f * w),
                         jnp.zeros((n, LANES - HYENA_EMB_DIM), F32)], axis=-1)
    w1p = jnp.concatenate([w1, jnp.zeros((LANES - HYENA_EMB_DIM, w1.shape[1]), F32)], axis=0)
    deltas = jnp.abs(jnp.linspace(math.log(HYENA_DECAY_TARGET) / HYENA_SLOW_DECAY_PCT,
                                  math.log(HYENA_DECAY_TARGET) / HYENA_FAST_DECAY_PCT, d, dtype=F32))
    order = w1.shape[1]
    tr = _row_tile(n, 256)
    full = lambda *shape: pl.BlockSpec(shape, lambda i: (0,) * len(shape))
    return pl.pallas_call(
        _hyena_filter_kernel,
        grid=(n // tr,),
        in_specs=[pl.BlockSpec((tr, LANES), lambda i: (i, 0)),
                  full(LANES, order), full(1, order), full(*w2.shape), full(w2.shape[0], 1, order),
                  full(1, order), full(order, 2 * d), full(1, d)],
        out_specs=[pl.BlockSpec((tr, d), lambda i: (i, 0))] * 2,
        out_shape=[jax.ShapeDtypeStruct((n, d), F32)] * 2,
        compiler_params=_params("arbitrary"),
        name="hyena_filters",
    )(z, w1p, b1.reshape(1, order), w2, b2.reshape(w2.shape[0], 1, order), freq.reshape(1, order),
      w_out, deltas.reshape(1, d))


def _dft_matrices(n):
    k = jnp.arange(n, dtype=jnp.int32)
    blk = min(n, LANES)

    def tables(t):
        ang = ((k[:, None] * t[None, :]) % (2 * n)).astype(F32) * (math.pi / n)
        return jnp.cos(ang), jnp.sin(ang)

    ca, sa = tables(jnp.arange(n // blk, dtype=jnp.int32) * blk)
    cb, sb = tables(jnp.arange(blk, dtype=jnp.int32))
    cmat = ca[:, :, None] * cb[:, None, :] - sa[:, :, None] * sb[:, None, :]
    smat = sa[:, :, None] * cb[:, None, :] + ca[:, :, None] * sb[:, None, :]
    return cmat.reshape(n, n).astype(BF16), smat.reshape(n, n).astype(BF16)


def _alt_sign(n):
    row = lax.broadcasted_iota(jnp.int32, (n, 1), 0)
    return jnp.where(row % 2 == 0, 1.0, -1.0).astype(F32), row


def _hyena_spec_kernel(c_ref, s_ref, hf_ref, hb_ref, kr_ref, ks_ref, kn_ref):
    n = hf_ref.shape[0]
    sgn, row = _alt_sign(n)
    hf = hf_ref[...]
    hb = jnp.where(row == 0, 0.0, hb_ref[...])
    wk = jnp.where(row == 0, 0.5 / n, 1.0 / n)
    kr_ref[...] = _dot(c_ref[...], (hf + hb).astype(BF16)) * wk
    ks_ref[...] = _dot(s_ref[...], (hf - hb).astype(BF16)) * wk
    kn_ref[...] = jnp.sum(sgn * (hf + hb), axis=0, keepdims=True) * (0.5 / n)


def _hyena_spectrum(cmat, smat, hf, hb):
    n, d = hf.shape
    assert n % 2 == 0
    cb = min(d, MXU_DIM)
    return pl.pallas_call(
        _hyena_spec_kernel,
        grid=(d // cb,),
        in_specs=[_resident((n, n), lambda j: (0, 0)), _resident((n, n), lambda j: (0, 0)),
                  pl.BlockSpec((n, cb), lambda j: (0, j)), pl.BlockSpec((n, cb), lambda j: (0, j))],
        out_specs=[pl.BlockSpec((n, cb), lambda j: (0, j)), pl.BlockSpec((n, cb), lambda j: (0, j)),
                   pl.BlockSpec((1, cb), lambda j: (0, j))],
        out_shape=[jax.ShapeDtypeStruct((n, d), F32), jax.ShapeDtypeStruct((n, d), F32),
                   jax.ShapeDtypeStruct((1, d), F32)],
        compiler_params=_params("arbitrary"),
        name="hyena_spectrum",
    )(cmat, smat, hf, hb)


def _hyena_conv_kernel(zx0_ref, zx1_ref, zv_ref, w0_ref, w1_ref, wv_ref, b0_ref, b1_ref, bv_ref,
                       c_ref, s_ref, kr_ref, ks_ref, kn_ref, db_ref, o_ref,
                       x0_scr, vv_scr, vb_scr, p_scr, q_scr):
    n = o_ref.shape[1]
    sgn, row = _alt_sign(n)

    def short_conv(z_ref, w_ref, b_ref):
        x = z_ref[0].astype(F32)
        prev = jnp.where(row == 0, 0.0, pltpu.roll(x, 1, axis=0))
        nxt = jnp.where(row == n - 1, 0.0, pltpu.roll(x, n - 1, axis=0))
        return w_ref[0:1, :] * prev + w_ref[1:2, :] * x + w_ref[2:3, :] * nxt + b_ref[...]

    x0_scr[...] = short_conv(zx0_ref, w0_ref, b0_ref)
    vv = short_conv(zv_ref, wv_ref, bv_ref) * short_conv(zx1_ref, w1_ref, b1_ref)
    vv_scr[...] = vv
    vb_scr[...] = vv.astype(BF16)
    nyq = jnp.sum(vv * sgn, axis=0, keepdims=True) * kn_ref[...]

    ch = min(n, 2 * MXU_DIM)
    for f in range(n // ch):
        sl = slice(f * ch, (f + 1) * ch)
        fr = _dot(c_ref[sl, :], vb_scr[...])
        fs = _dot(s_ref[sl, :], vb_scr[...])
        kr, ks = kr_ref[sl, :], ks_ref[sl, :]
        p_scr[sl, :] = (fr * kr - fs * ks).astype(BF16)
        q_scr[sl, :] = (fr * ks + fs * kr).astype(BF16)
    for t in range(n // ch):
        sl = slice(t * ch, (t + 1) * ch)
        y = _dot(c_ref[sl, :], p_scr[...]) + _dot(s_ref[sl, :], q_scr[...])
        y = y + sgn[sl, :] * nyq + vv_scr[sl, :] * db_ref[...]
        o_ref[0, sl, :] = (y * x0_scr[sl, :]).astype(o_ref.dtype)


def _hyena_conv(z, w_short, b_short, cmat, smat, kr, ks, kn, d_bias):
    bsz, n, d3 = z.shape
    d = d3 // 3
    cb = min(d, MXU_DIM)
    nb = d // cb
    zspec = lambda g: pl.BlockSpec((1, n, cb), lambda j, b: (b, 0, g * nb + j))
    wspec = lambda g: pl.BlockSpec((w_short.shape[0], cb), lambda j, b: (0, g * nb + j))
    bspec = lambda g: pl.BlockSpec((1, cb), lambda j, b: (0, g * nb + j))
    col = pl.BlockSpec((n, cb), lambda j, b: (0, j))
    vec = pl.BlockSpec((1, cb), lambda j, b: (0, j))
    bs2 = b_short.reshape(1, d3)
    return pl.pallas_call(
        _hyena_conv_kernel,
        grid=(nb, bsz),
        in_specs=[zspec(0), zspec(1), zspec(2), wspec(0), wspec(1), wspec(2), bspec(0), bspec(1), bspec(2),
                  _resident((n, n), lambda j, b: (0, 0)), _resident((n, n), lambda j, b: (0, 0)),
                  col, col, vec, vec],
        out_specs=pl.BlockSpec((1, n, cb), lambda j, b: (b, 0, j)),
        out_shape=jax.ShapeDtypeStruct((bsz, n, d), BF16),
        scratch_shapes=[pltpu.VMEM((n, cb), F32), pltpu.VMEM((n, cb), F32), pltpu.VMEM((n, cb), BF16),
                        pltpu.VMEM((n, cb), BF16), pltpu.VMEM((n, cb), BF16)],
        compiler_params=_params("arbitrary", "arbitrary"),
        name="hyena_conv",
    )(z, z, z, w_short, w_short, w_short, bs2, bs2, bs2, cmat, smat, kr, ks, kn, d_bias.reshape(1, d))


def _window_dft(nw):
    half = nw // 2
    t = jnp.arange(nw, dtype=jnp.int32)
    ang = ((jnp.arange(half, dtype=jnp.int32)[:, None] * t[None, :]) % nw).astype(F32) * (2.0 * math.pi / nw)
    nyq = jnp.where(t % 2 == 0, 1.0, -1.0).astype(F32)[None, :]
    return jnp.concatenate([jnp.cos(ang), nyq, jnp.sin(ang)[1:]], axis=0)


def _tap_spectrum_kernel(tr_ref, ts_ref, td_ref, w_ref, g_ref):
    hi = lax.Precision.HIGHEST
    for i, t_ref in enumerate((tr_ref, ts_ref, td_ref)):
        g_ref[i] = jnp.dot(t_ref[...], w_ref[...], precision=hi, preferred_element_type=F32)


def _tap_spectrum(w_dw, nw):
    width, d = w_dw.shape
    reach, half = width // 2, nw // 2
    kpad = -width % SUBLANES
    lag = reach - jnp.arange(width + kpad, dtype=jnp.int32)
    ang = ((jnp.arange(half, dtype=jnp.int32)[:, None] * lag[None, :]) % nw).astype(F32) * (2.0 * math.pi / nw)
    wf = jnp.where(jnp.arange(half) == 0, 1.0 / nw, 2.0 / nw).astype(F32)[:, None]
    tr = jnp.cos(ang) * wf
    ts = jnp.sin(ang) * wf
    nyq = jnp.where(lag % 2 == 0, 1.0 / nw, -1.0 / nw).astype(F32)[None, :]
    td = jnp.concatenate([nyq, tr[1:]], axis=0)
    wp = jnp.concatenate([w_dw, jnp.zeros((kpad, d), F32)], axis=0)
    full = lambda a: pl.BlockSpec(a.shape, lambda: (0,) * a.ndim)
    return pl.pallas_call(
        _tap_spectrum_kernel,
        in_specs=[full(tr), full(ts), full(td), full(wp)],
        out_specs=pl.BlockSpec((3, half, d), lambda: (0, 0, 0)),
        out_shape=jax.ShapeDtypeStruct((3, half, d), F32),
        name="tap_spectrum",
    )(tr, ts, td, wp)


def _conformer_post_kernel(x_ref, a_ref, ai_ref, gt_ref, bd_ref, lg_ref, lb_ref, w_ref, b_ref, h_ref, g_ref,
                           mod_ref, o_ref, xp_scr, ys_scr):
    n, d = x_ref.shape[1], x_ref.shape[2]
    nw = a_ref.shape[0]
    half = nw // 2
    tm = o_ref.shape[0]
    j = pl.program_id(1)

    @pl.when(j == 0)
    def _():
        xp_scr[0:CONV_HALO, :] = jnp.zeros((CONV_HALO, d), BF16)
        xp_scr[CONV_HALO:CONV_HALO + n, :] = x_ref[0]
        xp_scr[CONV_HALO + n:, :] = jnp.zeros((CONV_HALO, d), BF16)

    base = pl.multiple_of(j * tm, tm)
    f = _dot(a_ref[...], xp_scr[pl.ds(base, nw), :])
    xr, xs = f[:half], f[half:]
    ys_scr[0:half, :] = (xr * gt_ref[0] - xs * gt_ref[1]).astype(BF16)
    ys_scr[half:, :] = (xr * gt_ref[1] + xs * gt_ref[2]).astype(BF16)
    z = _dot(ai_ref[...], ys_scr[...]) + bd_ref[...]
    mu = jnp.mean(z, axis=-1, keepdims=True)
    zc = z - mu
    var = jnp.mean(zc * zc, axis=-1, keepdims=True)
    zn = zc * lax.rsqrt(var + LN_EPS) * lg_ref[...] + lb_ref[...]
    act = (zn / (1.0 + jnp.exp(-zn))).astype(BF16)
    y = _dot(act, w_ref[...]) + b_ref[...]
    o_ref[...] = h_ref[...] + mod_ref[0, 2:3, :] * _rms(y, g_ref[...])


def _conformer_post(x, w_dw, b_dw, ln_g, ln_b, w, b, h, g, mod, *, per_batch_mod, tm=256):
    bsz, n, d = x.shape
    assert w_dw.shape[0] // 2 < CONV_HALO
    tm = _row_tile(n, tm)
    per = n // tm
    nw = tm + 2 * CONV_HALO
    amat = _window_dft(nw)
    gtab = _tap_spectrum(w_dw, nw)
    vec = pl.BlockSpec((1, d), lambda bi, j: (0, 0))
    return pl.pallas_call(
        _conformer_post_kernel,
        grid=(bsz, per),
        in_specs=[pl.BlockSpec((1, n, d), lambda bi, j: (bi, 0, 0)),
                  _resident((nw, nw), lambda bi, j: (0, 0)), _resident((tm, nw), lambda bi, j: (0, 0)),
                  _resident((3, nw // 2, d), lambda bi, j: (0, 0, 0)), vec, vec, vec,
                  _resident((d, d), lambda bi, j: (0, 0)), vec,
                  pl.BlockSpec((tm, d), lambda bi, j: (bi * per + j, 0)), vec,
                  pl.BlockSpec((1, N_MOD, d), lambda bi, j: (bi if per_batch_mod else 0, 0, 0))],
        out_specs=pl.BlockSpec((tm, d), lambda bi, j: (bi * per + j, 0)),
        out_shape=jax.ShapeDtypeStruct((bsz * n, d), F32),
        scratch_shapes=[pltpu.VMEM((n + 2 * CONV_HALO, d), BF16), pltpu.VMEM((nw, d), BF16)],
        compiler_params=_params("arbitrary", "arbitrary"),
        name="conformer_post",
    )(x, amat.astype(BF16), amat.T[CONV_HALO:CONV_HALO + tm].astype(BF16), gtab, b_dw.reshape(1, d),
      ln_g.reshape(1, d), ln_b.reshape(1, d), w, b.reshape(1, d), h, g.reshape(1, d), mod)


def kernel(x, c, ctx, c_ctx, w_mod, b_mod, norm_mix_pre, norm_mix_post, norm_mlp_pre, norm_mlp_post, w_mlp_in, w_mlp_out, attn_w_qkv, attn_w_out, attn_lambda, attn_subln, hy_w_in, hy_b_in, hy_w_short, hy_b_short, hy_filt_w1, hy_filt_b1, hy_filt_w2, hy_filt_b2, hy_filt_freq, hy_filt_w_out, hy_bias, hy_w_out, hy_b_out, cv_w_pw1, cv_b_pw1, cv_w_dw, cv_b_dw, cv_ln_g, cv_ln_b, cv_w_pw2, cv_b_pw2):
    bsz, seq, d = x.shape
    n_ctx = ctx.shape[1]
    depth = w_mod.shape[0]
    assert bsz + 1 <= MOD_ROWS and d % (2 * MXU_DIM) == 0

    cc = jnp.concatenate([c, c_ctx[None, :], jnp.zeros((MOD_ROWS - bsz - 1, d), F32)], axis=0)
    mods = _mod_vectors(cc, w_mod, b_mod)
    rope_tabs = _rope_tables(seq)

    h_lat = x.reshape(bsz * seq, d)
    h_ctx = ctx.reshape(bsz * n_ctx, d)
    lat = dict(rows_per_mod=seq)
    cx = dict(rows_per_mod=bsz * n_ctx)
    for i in range(depth):
        last = i == depth - 1
        kind, j = i % N_MIXERS, i // N_MIXERS
        ctx_out = not last
        mod_l = mods[i, :bsz].reshape(bsz, N_MOD, d)
        mod_c = mods[i, bsz:bsz + 1].reshape(1, N_MOD, d)
        g_pre, g_post = norm_mix_pre[i], norm_mix_post[i]
        mix_l = mix_c = None
        if kind == 0:
            lam_init = 0.8 - 0.6 * math.exp(-0.3 * i)
            w_qkv, w_o = attn_w_qkv[j].astype(BF16), attn_w_out[j].astype(BF16)
            q_l, k_l, vt_l = _pre_project(h_lat, g_pre, mod_l, w_qkv, None, kind="attn", seq=seq,
                                          rope_tabs=rope_tabs, **lat)
            q_c, k_c, vt_c = _pre_project(h_ctx, g_pre, mod_c, w_qkv, None, kind="attn", seq=n_ctx, **cx)
            o_l = _diff_attention(q_l, [(k_c, vt_c), (k_l, vt_l)], attn_lambda[j], attn_subln[j], lam_init)
            mix_l = (o_l, w_o, None, g_post)
            if ctx_out:
                o_c = _diff_attention(q_c, [(k_c, vt_c)], attn_lambda[j], attn_subln[j], lam_init)
                mix_c = (o_c, w_o, None, g_post)
        elif kind == 1:
            w_in, w_o = hy_w_in[j].astype(BF16), hy_w_out[j].astype(BF16)

            def hyena(h, mod, n, rpm):
                z = _pre_project(h, g_pre, mod, w_in, hy_b_in[j], kind="hyena", rows_per_mod=rpm)
                hf, hb = _hyena_filters(n, hy_filt_w1[j], hy_filt_b1[j], hy_filt_w2[j], hy_filt_b2[j],
                                        hy_filt_freq[j], hy_filt_w_out[j], d)
                cmat, smat = _dft_matrices(n)
                kr, ks, kn = _hyena_spectrum(cmat, smat, hf, hb)
                g = _hyena_conv(z.reshape(bsz, n, 3 * d), hy_w_short[j], hy_b_short[j], cmat, smat,
                                kr, ks, kn, hy_bias[j])
                return (g.reshape(bsz * n, d), w_o, hy_b_out[j], g_post)

            mix_l = hyena(h_lat, mod_l, seq, seq)
            if ctx_out:
                mix_c = hyena(h_ctx, mod_c, n_ctx, bsz * n_ctx)
        else:
            w_1, w_2 = cv_w_pw1[j].astype(BF16), cv_w_pw2[j].astype(BF16)

            def conformer(h, mod, n, rpm, per_batch_mod):
                a = _pre_project(h, g_pre, mod, w_1, cv_b_pw1[j], kind="conformer", rows_per_mod=rpm)
                return _conformer_post(a.reshape(bsz, n, d), cv_w_dw[j], cv_b_dw[j], cv_ln_g[j], cv_ln_b[j],
                                       w_2, cv_b_pw2[j], h, g_post, mod, per_batch_mod=per_batch_mod)

            h_lat = conformer(h_lat, mod_l, seq, seq, True)
            if ctx_out:
                h_ctx = conformer(h_ctx, mod_c, n_ctx, bsz * n_ctx, False)
        w1, w2 = w_mlp_in[i].astype(BF16), w_mlp_out[i].astype(BF16)
        h_lat = _mlp_sublayer(h_lat, norm_mlp_pre[i], norm_mlp_post[i], mod_l, w1, w2, mix=mix_l, **lat)
        if ctx_out:
            h_ctx = _mlp_sublayer(h_ctx, norm_mlp_pre[i], norm_mlp_post[i], mod_c, w1, w2, mix=mix_c, **cx)
    return h_lat.reshape(bsz, seq, d)
```

```python
import functools
import math

import jax
import jax.numpy as jnp
from jax import lax
from jax.experimental import pallas as pl
from jax.experimental.pallas import tpu as pltpu

F32 = jnp.float32
BF16 = jnp.bfloat16

N_MIXERS = 3
N_MOD = 6
HEAD_DIM = 64
V_HEAD_DIM = 2 * HEAD_DIM
GRID_W = 64
ROPE_THETA = 10000.0
NORM_EPS = 1e-6
LN_EPS = 1e-5
HYENA_EMB_DIM = 33
HYENA_DECAY_TARGET = 1e-2
HYENA_FAST_DECAY_PCT = 0.3
HYENA_SLOW_DECAY_PCT = 1.5

LANES = 128
SUBLANES = 8
MXU_DIM = 256
VMEM_LIMIT_BYTES = 56 * 1024 * 1024
MOD_ROWS = 24
CONV_HALO = 16
HYENA_BLOCK = 512


def _params(*sem):
    return pltpu.CompilerParams(dimension_semantics=sem, vmem_limit_bytes=VMEM_LIMIT_BYTES)


def _resident(shape, index_map):
    return pl.BlockSpec(shape, index_map, pipeline_mode=pl.Buffered(1))


def _row_tile(rows, want):
    t = min(rows, want)
    assert rows % t == 0
    return t


def _rms(x, g):
    return x * lax.rsqrt(jnp.mean(x * x, axis=-1, keepdims=True) + NORM_EPS) * g


def _dot(a, b):
    return jnp.dot(a, b, preferred_element_type=F32)


def _mod_kernel(c_ref, w_ref, b_ref, o_ref):
    c = c_ref[...]
    s = (c / (1.0 + jnp.exp(-c))).astype(BF16)
    o_ref[0] = _dot(s, w_ref[0].astype(BF16)) + b_ref[0]


def _mod_vectors(cc, w_mod, b_mod):
    depth, d, n = w_mod.shape
    tn = 1024
    return pl.pallas_call(
        _mod_kernel,
        grid=(depth, n // tn),
        in_specs=[pl.BlockSpec((MOD_ROWS, d), lambda i, j: (0, 0)),
                  pl.BlockSpec((1, d, tn), lambda i, j: (i, 0, j)),
                  pl.BlockSpec((1, 1, tn), lambda i, j: (i, 0, j))],
        out_specs=pl.BlockSpec((1, MOD_ROWS, tn), lambda i, j: (i, 0, j)),
        out_shape=jax.ShapeDtypeStruct((depth, MOD_ROWS, n), F32),
        compiler_params=_params("arbitrary", "arbitrary"),
        name="mod_vectors",
    )(cc, w_mod, b_mod.reshape(depth, 1, n))


def _rope_group(z, cos, sin, first_half):
    partner = jnp.where(first_half, pltpu.roll(z, LANES - 16, axis=1), pltpu.roll(z, 16, axis=1))
    return z * cos + partner * sin


def _pre_kernel(*refs, kind, d, rope, per):
    h_ref, g_ref, mod_ref, w_ref = refs[:4]
    rest = list(refs[4:])
    b_ref = rest.pop(0) if kind != "attn" else None
    cos_ref, sin_ref = (rest.pop(0), rest.pop(0)) if rope else (None, None)
    if kind == "attn":
        oq_ref, ok_ref, ovt_ref = rest
    elif kind == "hyena":
        hp_ref, hn_ref, ws_ref, bs_ref, ox0_ref, ovv_ref = rest
    else:
        (o_ref,) = rest

    def modulated(x):
        return _rms(x, g_ref[...]) * (1.0 + mod_ref[0, 1:2, :]) + mod_ref[0, 0:1, :]

    u = modulated(h_ref[...])
    ub = u.astype(BF16)
    nc = 2 * MXU_DIM
    if kind == "hyena":
        tm = u.shape[0]
        i = pl.program_id(0)
        rowx = lax.broadcasted_iota(jnp.int32, (tm + 2 * SUBLANES, 1), 0)
        outside = ((rowx < SUBLANES) & (i % per == 0)) | ((rowx >= tm + SUBLANES) & (i % per == per - 1))
        keep = jnp.where(outside, 0.0, 1.0)
        u_ext = jnp.concatenate([modulated(hp_ref[...]), u, modulated(hn_ref[...])], axis=0).astype(BF16)

        def conv_cols(c0):
            cols = slice(c0, c0 + nc)
            z = (_dot(u_ext, w_ref[:, cols]) + b_ref[:, cols]) * keep
            prev = pltpu.roll(z, 1, axis=0)[SUBLANES:SUBLANES + tm]
            nxt = pltpu.roll(z, tm + 2 * SUBLANES - 1, axis=0)[SUBLANES:SUBLANES + tm]
            return (ws_ref[0:1, cols] * prev + ws_ref[1:2, cols] * z[SUBLANES:SUBLANES + tm]
                    + ws_ref[2:3, cols] * nxt + bs_ref[:, cols])

        for c in range(d // nc):
            ox0_ref[:, c * nc:(c + 1) * nc] = conv_cols(c * nc).astype(ox0_ref.dtype)
            ovv_ref[:, c * nc:(c + 1) * nc] = (conv_cols(2 * d + c * nc) * conv_cols(d + c * nc)).astype(ovv_ref.dtype)
        return
    if kind == "conformer":
        for c in range(d // nc):
            a = _dot(ub, w_ref[:, c * nc:(c + 1) * nc]) + b_ref[:, c * nc:(c + 1) * nc]
            g = _dot(ub, w_ref[:, d + c * nc:d + (c + 1) * nc]) + b_ref[:, d + c * nc:d + (c + 1) * nc]
            o_ref[:, c * nc:(c + 1) * nc] = (a / (1.0 + jnp.exp(-g))).astype(o_ref.dtype)
        return
    if rope:
        lane = lax.broadcasted_iota(jnp.int32, (1, LANES), 1)
        first_half = (lane % 32) < 16
        cos, sin = cos_ref[...], sin_ref[...]
    for c in range(3 * d // nc):
        z = _dot(ub, w_ref[:, c * nc:(c + 1) * nc])
        section = c * nc // d
        for gi in range(nc // LANES):
            head = (c * nc - section * d) // LANES + gi
            zg = z[:, gi * LANES:(gi + 1) * LANES]
            if rope and section < 2:
                zg = _rope_group(zg, cos, sin, first_half)
            if section == 0:
                oq_ref[0, head] = (zg * (HEAD_DIM ** -0.5 * math.log2(math.e))).astype(oq_ref.dtype)
            elif section == 1:
                ok_ref[0, head] = zg.astype(ok_ref.dtype)
            else:
                ovt_ref[0, head] = zg.T.astype(ovt_ref.dtype)


def _pre_project(h, g, mod, w, b, *, kind, rows_per_mod, seq=None, rope_tabs=None, short=None, tm=512):
    rows, d = h.shape
    n_in = w.shape[1]
    n_out = d if kind == "conformer" else n_in
    tm = _row_tile(min(rows, rows_per_mod, seq or rows), tm)
    rope = rope_tabs is not None
    per = seq // tm if seq else None
    if kind == "attn":
        nh = d // V_HEAD_DIM
        qk_spec = pl.BlockSpec((1, nh, tm, V_HEAD_DIM), lambda i: (i // per, 0, i % per, 0))
        out_specs = [qk_spec, qk_spec, pl.BlockSpec((1, nh, V_HEAD_DIM, tm), lambda i: (i // per, 0, 0, i % per))]
        qk_shape = jax.ShapeDtypeStruct((rows // seq, nh, seq, V_HEAD_DIM), BF16)
        out_shape = [qk_shape, qk_shape, jax.ShapeDtypeStruct((rows // seq, nh, V_HEAD_DIM, seq), BF16)]
    elif kind == "hyena":
        out_specs = [pl.BlockSpec((tm, d), lambda i: (i, 0))] * 2
        out_shape = [jax.ShapeDtypeStruct((rows, d), BF16)] * 2
    else:
        out_specs = pl.BlockSpec((tm, n_out), lambda i: (i, 0))
        out_shape = jax.ShapeDtypeStruct((rows, n_out), BF16)
    ins = [h, g.reshape(1, d), mod, w]
    specs = [pl.BlockSpec((tm, d), lambda i: (i, 0)),
             pl.BlockSpec((1, d), lambda i: (0, 0)),
             pl.BlockSpec((1, N_MOD, d), lambda i: (i * tm // rows_per_mod, 0, 0)),
             _resident((d, n_in), lambda i: (0, 0))]
    if kind != "attn":
        ins.append(b.reshape(1, n_in))
        specs.append(pl.BlockSpec((1, n_in), lambda i: (0, 0)))
    if rope:
        assert rope_tabs[0].shape[0] == seq
        ins += list(rope_tabs)
        specs += [pl.BlockSpec((tm, LANES), lambda i: (i % per, 0))] * 2
    if kind == "hyena":
        w_short, b_short = short
        assert w_short.shape[0] == 3 and tm % SUBLANES == 0
        groups, last = tm // SUBLANES, rows // SUBLANES - 1
        ins += [h, h, w_short, b_short.reshape(1, n_in)]
        specs += [pl.BlockSpec((SUBLANES, d), lambda i: (jnp.maximum(i * groups - 1, 0), 0)),
                  pl.BlockSpec((SUBLANES, d), lambda i: (jnp.minimum((i + 1) * groups, last), 0)),
                  pl.BlockSpec((3, n_in), lambda i: (0, 0)), pl.BlockSpec((1, n_in), lambda i: (0, 0))]
    return pl.pallas_call(
        functools.partial(_pre_kernel, kind=kind, d=d, rope=rope, per=per),
        grid=(rows // tm,),
        in_specs=specs,
        out_specs=out_specs,
        out_shape=out_shape,
        compiler_params=_params("arbitrary"),
        name=f"pre_{kind}",
    )(*ins)


def _rope_tables(seq):
    t = jnp.arange(seq)
    pos = jnp.stack([(t // GRID_W).astype(F32), (t % GRID_W).astype(F32)], axis=1)
    half = HEAD_DIM // 4
    inv_freq = ROPE_THETA ** (-jnp.arange(half, dtype=F32) / half)
    ang = pos[:, :, None] * inv_freq[None, None, :]
    cos = jnp.concatenate([jnp.cos(ang), jnp.cos(ang)], axis=-1).reshape(seq, HEAD_DIM)
    sin = jnp.concatenate([-jnp.sin(ang), jnp.sin(ang)], axis=-1).reshape(seq, HEAD_DIM)
    rep = LANES // HEAD_DIM
    return jnp.tile(cos, (1, rep)), jnp.tile(sin, (1, rep))


def _attn_kernel(*refs, n_kv, lam_init, tq):
    lam_ref, sub_ref, q_ref = refs[:3]
    k_refs = refs[3:3 + 2 * n_kv:2]
    vt_refs = refs[4:4 + 2 * n_kv:2]
    o_ref, st_scr, acc_scr, den_scr = refs[3 + 2 * n_kv:]
    lk_all = st_scr.shape[1]
    qc = st_scr.shape[2] // 2
    kp = min(MXU_DIM, *(k_ref.shape[2] for k_ref in k_refs))
    parts = [(k_ref, vt_ref, o) for k_ref, vt_ref in zip(k_refs, vt_refs) for o in range(0, k_ref.shape[2], kp)]
    n_parts = len(parts)
    assert n_parts * kp == lk_all

    lv = lam_ref[...]
    lam = (jnp.exp(jnp.sum(lv[0:1] * lv[1:2], axis=1, keepdims=True))
           - jnp.exp(jnp.sum(lv[2:3] * lv[3:4], axis=1, keepdims=True)) + lam_init)
    lane = lax.broadcasted_iota(jnp.int32, (1, V_HEAD_DIM), 1)

    def stacked_queries(c):
        q = q_ref[0, 0, pl.ds(pl.multiple_of(c * qc, qc), qc), :]
        zero = jnp.zeros_like(q)
        return jnp.concatenate([jnp.where(lane < HEAD_DIM, q, zero), jnp.where(lane >= HEAD_DIM, q, zero)], axis=0)

    def scores_part(qq, slot, p):
        k_ref, _, off = parts[p]
        st = lax.dot_general(k_ref[0, 0, off:off + kp, :], qq, (((1,), (1,)), ((), ())),
                             preferred_element_type=F32)
        st_scr[slot, p * kp:(p + 1) * kp, :] = st
        return jnp.max(st, axis=0, keepdims=True)

    def scores(c, slot):
        qq = stacked_queries(c)
        return functools.reduce(jnp.maximum, [scores_part(qq, slot, p) for p in range(n_parts)])

    def finish(c, slot):
        acc = acc_scr[slot]
        r = 1.0 / den_scr[slot]
        ot = acc[:, :qc] * r[:, :qc] - acc[:, qc:] * (lam * r[:, qc:])
        o = _rms(ot.T, sub_ref[...]) * (1.0 - lam_init)
        o_ref[0, 0, pl.ds(pl.multiple_of(c * qc, qc), qc), :] = o.astype(o_ref.dtype)

    def chunk(c, slot, m, nxt, prev):
        qq_next = stacked_queries(c + 1) if nxt else None
        m_next = denom = acc = None
        for p in range(n_parts):
            if nxt:
                pm = scores_part(qq_next, 1 - slot, p)
                m_next = pm if m_next is None else jnp.maximum(m_next, pm)
            if prev and p == 0:
                finish(c - 1, 1 - slot)
            e = jnp.exp2(st_scr[slot, p * kp:(p + 1) * kp, :] - m)
            ds = jnp.sum(e, axis=0, keepdims=True)
            _, vt_ref, off = parts[p]
            da = _dot(vt_ref[0, 0, :, off:off + kp], e.astype(BF16))
            denom = ds if denom is None else denom + ds
            acc = da if acc is None else acc + da
        acc_scr[slot] = acc
        den_scr[slot] = denom
        return m_next

    n_chunks = tq // qc
    assert n_chunks == 1 or n_chunks % 2 == 0
    m = scores(0, 0)
    if n_chunks > 1:
        m = chunk(0, 0, m, True, False)

        def pair(i, m):
            return chunk(2 * i + 2, 0, chunk(2 * i + 1, 1, m, True, True), True, True)
        m = lax.fori_loop(0, n_chunks // 2 - 1, pair, m)
        chunk(n_chunks - 1, 1, m, False, True)
        finish(n_chunks - 1, 1)
    else:
        chunk(0, 0, m, False, False)
        finish(0, 0)


def _diff_attention(q, kv_srcs, lam_vec, subln, lam_init, tq=2048, qc=256):
    bsz, n_heads, lq, _ = q.shape
    tq = _row_tile(lq, tq)
    qc = _row_tile(tq, qc)
    lk_all = sum(k.shape[2] for k, _ in kv_srcs)
    ins = [lam_vec, subln.reshape(1, V_HEAD_DIM), q]
    specs = [pl.BlockSpec(lam_vec.shape, lambda b, h, i: (0, 0)),
             pl.BlockSpec((1, V_HEAD_DIM), lambda b, h, i: (0, 0)),
             pl.BlockSpec((1, 1, tq, V_HEAD_DIM), lambda b, h, i: (b, h, i, 0))]
    for k, vt in kv_srcs:
        lk = k.shape[2]
        ins += [k, vt]
        specs += [pl.BlockSpec((1, 1, lk, V_HEAD_DIM), lambda b, h, i: (b, h, 0, 0)),
                  pl.BlockSpec((1, 1, V_HEAD_DIM, lk), lambda b, h, i: (b, h, 0, 0))]
    return pl.pallas_call(
        functools.partial(_attn_kernel, n_kv=len(kv_srcs), lam_init=lam_init, tq=tq),
        grid=(bsz, n_heads, lq // tq),
        in_specs=specs,
        out_specs=pl.BlockSpec((1, 1, tq, V_HEAD_DIM), lambda b, h, i: (b, h, i, 0)),
        out_shape=jax.ShapeDtypeStruct((bsz, n_heads, lq, V_HEAD_DIM), BF16),
        scratch_shapes=[pltpu.VMEM((2, lk_all, 2 * qc), F32), pltpu.VMEM((2, V_HEAD_DIM, 2 * qc), F32),
                        pltpu.VMEM((2, 1, 2 * qc), F32)],
        compiler_params=_params("arbitrary", "arbitrary", "arbitrary"),
        name="diff_attention",
    )(*ins)


def _mlp_kernel(*refs, mix, mix_bias):
    refs = list(refs)
    if mix:
        y_ref, wo_ref = refs.pop(0), refs.pop(0)
        bo_ref = refs.pop(0) if mix_bias else None
        gmix_ref = refs.pop(0)
    h_ref, gpre_ref, gpost_ref, mod_ref, w1_ref, w2_ref, o_ref, hid_ref = refs
    x = h_ref[...]
    if mix:
        if len(y_ref.shape) == 4:
            yin = jnp.concatenate([y_ref[0, hd] for hd in range(y_ref.shape[1])], axis=1)
        else:
            yin = y_ref[...]
        y = _dot(yin, wo_ref[...])
        if mix_bias:
            y = y + bo_ref[...]
        x = x + mod_ref[0, 2:3, :] * _rms(y, gmix_ref[...])
    v = (_rms(x, gpre_ref[...]) * (1.0 + mod_ref[0, 4:5, :]) + mod_ref[0, 3:4, :]).astype(BF16)
    dff = w1_ref.shape[1]
    fc = 4 * MXU_DIM
    for c in range(dff // fc):
        hid = jnp.maximum(_dot(v, w1_ref[:, c * fc:(c + 1) * fc]), 0.0)
        hid_ref[:, c * fc:(c + 1) * fc] = (hid * hid).astype(BF16)
    y = _dot(hid_ref[...], w2_ref[...])
    o_ref[...] = x + mod_ref[0, 5:6, :] * _rms(y, gpost_ref[...])


def _mlp_sublayer(h, gpre, gpost, mod, w1, w2, *, rows_per_mod, mix=None, tm=512):
    rows, d = h.shape
    dff = w1.shape[1]
    head_major = mix is not None and mix[0].ndim == 4
    tm = _row_tile(min(rows, rows_per_mod, mix[0].shape[2] if head_major else rows), tm)
    tile = pl.BlockSpec((tm, d), lambda i: (i, 0))
    vec = pl.BlockSpec((1, d), lambda i: (0, 0))
    ins, specs, mix_bias = [], [], False
    if mix is not None:
        y, wo, bo, gmix = mix
        mix_bias = bo is not None
        ins = [y, wo] + ([bo.reshape(1, d)] if mix_bias else []) + [gmix.reshape(1, d)]
        if head_major:
            per = y.shape[2] // tm
            y_spec = pl.BlockSpec((1, y.shape[1], tm, V_HEAD_DIM), lambda i: (i // per, 0, i % per, 0))
        else:
            y_spec = tile
        specs = [y_spec, _resident((d, d), lambda i: (0, 0))] + ([vec] if mix_bias else []) + [vec]
    return pl.pallas_call(
        functools.partial(_mlp_kernel, mix=mix is not None, mix_bias=mix_bias),
        grid=(rows // tm,),
        in_specs=specs + [tile, vec, vec,
                          pl.BlockSpec((1, N_MOD, d), lambda i: (i * tm // rows_per_mod, 0, 0)),
                          _resident((d, dff), lambda i: (0, 0)),
                          _resident((dff, d), lambda i: (0, 0))],
        out_specs=tile,
        out_shape=jax.ShapeDtypeStruct((rows, d), F32),
        scratch_shapes=[pltpu.VMEM((tm, dff), BF16)],
        compiler_params=_params("arbitrary"),
        name="mlp_sublayer",
    )(*ins, h, gpre.reshape(1, d), gpost.reshape(1, d), mod, w1, w2)


def _hyena_filter_kernel(z_ref, w1_ref, b1_ref, w2_ref, b2_ref, fr_ref, wo_ref, dl_ref, hf_ref, hb_ref):
    hi = lax.Precision.HIGHEST
    z = z_ref[...]
    freq = fr_ref[...]
    hdn = jnp.sin(freq * (jnp.dot(z, w1_ref[...], precision=hi, preferred_element_type=F32) + b1_ref[...]))
    for j in range(w2_ref.shape[0]):
        hdn = jnp.sin(freq * (jnp.dot(hdn, w2_ref[j], precision=hi, preferred_element_type=F32) + b2_ref[j]))
    h = jnp.dot(hdn, wo_ref[...], precision=hi, preferred_element_type=F32)
    d = hf_ref.shape[1]
    window = jnp.exp(-z[:, 0:1] * dl_ref[...])
    hf_ref[...] = h[:, :d] * window
    hb_ref[...] = h[:, d:] * window


def _hyena_filters(n, w1, b1, w2, b2, freq, w_out, d):
    bands = (HYENA_EMB_DIM - 1) // 2
    t = jnp.linspace(0.0, 1.0, n, dtype=F32)[:, None]
    w = 2.0 * math.pi * jnp.arange(n, dtype=F32)[:, None] / n
    f = jnp.linspace(1e-4, bands - 1, bands, dtype=F32)[None, :]
    z = jnp.concatenate([t, jnp.cos(f * w), -jnp.sin(f * w),
                         jnp.zeros((n, LANES - HYENA_EMB_DIM), F32)], axis=-1)
    w1p = jnp.concatenate([w1, jnp.zeros((LANES - HYENA_EMB_DIM, w1.shape[1]), F32)], axis=0)
    deltas = jnp.abs(jnp.linspace(math.log(HYENA_DECAY_TARGET) / HYENA_SLOW_DECAY_PCT,
                                  math.log(HYENA_DECAY_TARGET) / HYENA_FAST_DECAY_PCT, d, dtype=F32))
    order = w1.shape[1]
    tr = _row_tile(n, 256)
    full = lambda *shape: pl.BlockSpec(shape, lambda i: (0,) * len(shape))
    return pl.pallas_call(
        _hyena_filter_kernel,
        grid=(n // tr,),
        in_specs=[pl.BlockSpec((tr, LANES), lambda i: (i, 0)),
                  full(LANES, order), full(1, order), full(*w2.shape), full(w2.shape[0], 1, order),
                  full(1, order), full(order, 2 * d), full(1, d)],
        out_specs=[pl.BlockSpec((tr, d), lambda i: (i, 0))] * 2,
        out_shape=[jax.ShapeDtypeStruct((n, d), F32)] * 2,
        compiler_params=_params("arbitrary"),
        name="hyena_filters",
    )(z, w1p, b1.reshape(1, order), w2, b2.reshape(w2.shape[0], 1, order), freq.reshape(1, order),
      w_out, deltas.reshape(1, d))


def _dft_matrices(n):
    k = jnp.arange(n, dtype=jnp.int32)
    blk = min(n, LANES)

    def tables(t):
        ang = ((k[:, None] * t[None, :]) % (2 * n)).astype(F32) * (math.pi / n)
        return jnp.cos(ang), jnp.sin(ang)

    ca, sa = tables(jnp.arange(n // blk, dtype=jnp.int32) * blk)
    cb, sb = tables(jnp.arange(blk, dtype=jnp.int32))
    cmat = ca[:, :, None] * cb[:, None, :] - sa[:, :, None] * sb[:, None, :]
    smat = sa[:, :, None] * cb[:, None, :] + ca[:, :, None] * sb[:, None, :]
    return cmat.reshape(n, n).astype(BF16), smat.reshape(n, n).astype(BF16)


def _alt_sign(n):
    row = lax.broadcasted_iota(jnp.int32, (n, 1), 0)
    return jnp.where(row % 2 == 0, 1.0, -1.0).astype(F32), row


def _hyena_spec_kernel(c_ref, s_ref, hf_ref, hb_ref, kr_ref, ks_ref, kn_ref):
    m = c_ref.shape[0]
    n = hf_ref.shape[0]
    nb = n // m
    sgn, row = _alt_sign(m)
    wk = jnp.where(row == 0, 0.5 / m, 1.0 / m)

    def transforms(seg):
        sb = seg.astype(BF16)
        first = sb[0:1, :].astype(F32)
        ct, st = _dot(c_ref[...], sb), _dot(s_ref[...], sb)
        at = jnp.sum(sgn * seg, axis=0, keepdims=True)
        return ct, st, at, ct - first, at - seg[0:1, :]

    fwd = [transforms(hf_ref[j * m:(j + 1) * m, :]) for j in range(nb)]
    bwd = [transforms(hb_ref[j * m:(j + 1) * m, :]) for j in range(nb)]
    for lag in range(-(nb - 1), nb):
        if lag == 0:
            kr, ks, kn = fwd[0][0] + bwd[0][3], fwd[0][1] - bwd[0][1], fwd[0][2] + bwd[0][4]
        else:
            cur, prv = (fwd[lag], fwd[lag - 1]) if lag > 0 else (bwd[-lag], bwd[-lag - 1])
            kr = cur[0] + sgn * prv[3]
            ks = cur[1] + sgn * prv[1]
            kn = cur[2] + prv[4]
            if lag < 0:
                ks = -ks
        kr_ref[lag + nb - 1] = kr * wk
        ks_ref[lag + nb - 1] = ks * wk
        kn_ref[lag + nb - 1] = kn * (0.5 / m)


def _hyena_spectrum(cmat, smat, hf, hb):
    n, d = hf.shape
    m = cmat.shape[0]
    assert n % m == 0 and m % 2 == 0
    nl = 2 * (n // m) - 1
    cb = min(d, MXU_DIM)
    return pl.pallas_call(
        _hyena_spec_kernel,
        grid=(d // cb,),
        in_specs=[_resident((m, m), lambda j: (0, 0)), _resident((m, m), lambda j: (0, 0)),
                  pl.BlockSpec((n, cb), lambda j: (0, j)), pl.BlockSpec((n, cb), lambda j: (0, j))],
        out_specs=[pl.BlockSpec((nl, m, cb), lambda j: (0, 0, j)), pl.BlockSpec((nl, m, cb), lambda j: (0, 0, j)),
                   pl.BlockSpec((nl, 1, cb), lambda j: (0, 0, j))],
        out_shape=[jax.ShapeDtypeStruct((nl, m, d), F32), jax.ShapeDtypeStruct((nl, m, d), F32),
                   jax.ShapeDtypeStruct((nl, 1, d), F32)],
        compiler_params=_params("arbitrary"),
        name="hyena_spectrum",
    )(cmat, smat, hf, hb)


def _hyena_conv_kernel(x0_ref, vv_ref, c_ref, s_ref, kr_ref, ks_ref, kn_ref, db_ref, o_ref,
                       vb_scr, fr_scr, fs_scr, p_scr, q_scr):
    n, cb = o_ref.shape[1], o_ref.shape[2]
    m = c_ref.shape[0]
    nb = n // m
    sgn, _ = _alt_sign(m)

    fn = []
    for jb in range(nb):
        blk = vv_ref[0, jb * m:(jb + 1) * m, :]
        vb_scr[:, jb * cb:(jb + 1) * cb] = blk
        fn.append(jnp.sum(blk.astype(F32) * sgn, axis=0, keepdims=True))
    fr_scr[...] = _dot(c_ref[...], vb_scr[...])
    fs_scr[...] = _dot(s_ref[...], vb_scr[...])

    rc = min(m, 32)
    nyq = []
    for ib in range(nb):
        nyq.append(functools.reduce(jnp.add, [fn[jb] * kn_ref[ib - jb + nb - 1] for jb in range(nb)]))
        for r0 in range(0, m, rc):
            rows = slice(r0, r0 + rc)
            p = q = None
            for jb in range(nb):
                cols = slice(jb * cb, (jb + 1) * cb)
                fr, fs = fr_scr[rows, cols], fs_scr[rows, cols]
                kr, ks = kr_ref[ib - jb + nb - 1, rows, :], ks_ref[ib - jb + nb - 1, rows, :]
                dp, dq = fr * kr - fs * ks, fr * ks + fs * kr
                p, q = (dp, dq) if p is None else (p + dp, q + dq)
            p_scr[rows, ib * cb:(ib + 1) * cb] = p.astype(BF16)
            q_scr[rows, ib * cb:(ib + 1) * cb] = q.astype(BF16)
    y = _dot(c_ref[...], p_scr[...]) + _dot(s_ref[...], q_scr[...])
    for ib in range(nb):
        rows = slice(ib * m, (ib + 1) * m)
        yb = y[:, ib * cb:(ib + 1) * cb] + sgn * nyq[ib] + vv_ref[0, rows, :].astype(F32) * db_ref[...]
        o_ref[0, rows, :] = (yb * x0_ref[0, rows, :].astype(F32)).astype(o_ref.dtype)


def _hyena_conv(x0, vv, cmat, smat, kr, ks, kn, d_bias):
    bsz, n, d = vv.shape
    cb = min(d, MXU_DIM)
    nb = d // cb
    m, nl = cmat.shape[0], kr.shape[0]
    wide = (m, (n // m) * cb)
    seq = pl.BlockSpec((1, n, cb), lambda j, b: (b, 0, j))
    spec = pl.BlockSpec((nl, m, cb), lambda j, b: (0, 0, j))
    nyq = pl.BlockSpec((nl, 1, cb), lambda j, b: (0, 0, j))
    vec = pl.BlockSpec((1, cb), lambda j, b: (0, j))
    return pl.pallas_call(
        _hyena_conv_kernel,
        grid=(nb, bsz),
        in_specs=[seq, seq, _resident((m, m), lambda j, b: (0, 0)), _resident((m, m), lambda j, b: (0, 0)),
                  spec, spec, nyq, vec],
        out_specs=seq,
        out_shape=jax.ShapeDtypeStruct((bsz, n, d), BF16),
        scratch_shapes=[pltpu.VMEM(wide, BF16), pltpu.VMEM(wide, F32), pltpu.VMEM(wide, F32),
                        pltpu.VMEM(wide, BF16), pltpu.VMEM(wide, BF16)],
        compiler_params=_params("arbitrary", "arbitrary"),
        name="hyena_conv",
    )(x0, vv, cmat, smat, kr, ks, kn, d_bias.reshape(1, d))


def _window_dft(nw):
    half = nw // 2
    t = jnp.arange(nw, dtype=jnp.int32)
    ang = ((jnp.arange(half, dtype=jnp.int32)[:, None] * t[None, :]) % nw).astype(F32) * (2.0 * math.pi / nw)
    nyq = jnp.where(t % 2 == 0, 1.0, -1.0).astype(F32)[None, :]
    return jnp.concatenate([jnp.cos(ang), nyq, jnp.sin(ang)[1:]], axis=0)


def _tap_spectrum_kernel(tr_ref, ts_ref, td_ref, w_ref, g_ref):
    hi = lax.Precision.HIGHEST
    for i, t_ref in enumerate((tr_ref, ts_ref, td_ref)):
        g_ref[i] = jnp.dot(t_ref[...], w_ref[...], precision=hi, preferred_element_type=F32)


def _tap_spectrum(w_dw, nw):
    width, d = w_dw.shape
    reach, half = width // 2, nw // 2
    kpad = -width % SUBLANES
    lag = reach - jnp.arange(width + kpad, dtype=jnp.int32)
    ang = ((jnp.arange(half, dtype=jnp.int32)[:, None] * lag[None, :]) % nw).astype(F32) * (2.0 * math.pi / nw)
    wf = jnp.where(jnp.arange(half) == 0, 1.0 / nw, 2.0 / nw).astype(F32)[:, None]
    tr = jnp.cos(ang) * wf
    ts = jnp.sin(ang) * wf
    nyq = jnp.where(lag % 2 == 0, 1.0 / nw, -1.0 / nw).astype(F32)[None, :]
    td = jnp.concatenate([nyq, tr[1:]], axis=0)
    wp = jnp.concatenate([w_dw, jnp.zeros((kpad, d), F32)], axis=0)
    full = lambda a: pl.BlockSpec(a.shape, lambda: (0,) * a.ndim)
    return pl.pallas_call(
        _tap_spectrum_kernel,
        in_specs=[full(tr), full(ts), full(td), full(wp)],
        out_specs=pl.BlockSpec((3, half, d), lambda: (0, 0, 0)),
        out_shape=jax.ShapeDtypeStruct((3, half, d), F32),
        name="tap_spectrum",
    )(tr, ts, td, wp)


def _conformer_post_kernel(x_ref, a_ref, ai_ref, gt_ref, bd_ref, lg_ref, lb_ref, w_ref, b_ref, h_ref, g_ref,
                           mod_ref, o_ref, xp_scr, ys_scr):
    n, d = x_ref.shape[1], x_ref.shape[2]
    nw = a_ref.shape[0]
    half = nw // 2
    tm = o_ref.shape[0]
    j = pl.program_id(1)

    @pl.when(j == 0)
    def _():
        xp_scr[0:CONV_HALO, :] = jnp.zeros((CONV_HALO, d), BF16)
        xp_scr[CONV_HALO:CONV_HALO + n, :] = x_ref[0]
        xp_scr[CONV_HALO + n:, :] = jnp.zeros((CONV_HALO, d), BF16)

    base = pl.multiple_of(j * tm, tm)
    f = _dot(a_ref[...], xp_scr[pl.ds(base, nw), :])
    xr, xs = f[:half], f[half:]
    ys_scr[0:half, :] = (xr * gt_ref[0] - xs * gt_ref[1]).astype(BF16)
    ys_scr[half:, :] = (xr * gt_ref[1] + xs * gt_ref[2]).astype(BF16)
    z = _dot(ai_ref[...], ys_scr[...]) + bd_ref[...]
    mu = jnp.mean(z, axis=-1, keepdims=True)
    zc = z - mu
    var = jnp.mean(zc * zc, axis=-1, keepdims=True)
    zn = zc * lax.rsqrt(var + LN_EPS) * lg_ref[...] + lb_ref[...]
    act = (zn / (1.0 + jnp.exp(-zn))).astype(BF16)
    y = _dot(act, w_ref[...]) + b_ref[...]
    o_ref[...] = h_ref[...] + mod_ref[0, 2:3, :] * _rms(y, g_ref[...])


def _conformer_post(x, w_dw, b_dw, ln_g, ln_b, w, b, h, g, mod, *, per_batch_mod, tm=256):
    bsz, n, d = x.shape
    assert w_dw.shape[0] // 2 < CONV_HALO
    tm = _row_tile(n, tm)
    per = n // tm
    nw = tm + 2 * CONV_HALO
    amat = _window_dft(nw)
    gtab = _tap_spectrum(w_dw, nw)
    vec = pl.BlockSpec((1, d), lambda bi, j: (0, 0))
    return pl.pallas_call(
        _conformer_post_kernel,
        grid=(bsz, per),
        in_specs=[pl.BlockSpec((1, n, d), lambda bi, j: (bi, 0, 0)),
                  _resident((nw, nw), lambda bi, j: (0, 0)), _resident((tm, nw), lambda bi, j: (0, 0)),
                  _resident((3, nw // 2, d), lambda bi, j: (0, 0, 0)), vec, vec, vec,
                  _resident((d, d), lambda bi, j: (0, 0)), vec,
                  pl.BlockSpec((tm, d), lambda bi, j: (bi * per + j, 0)), vec,
                  pl.BlockSpec((1, N_MOD, d), lambda bi, j: (bi if per_batch_mod else 0, 0, 0))],
        out_specs=pl.BlockSpec((tm, d), lambda bi, j: (bi * per + j, 0)),
        out_shape=jax.ShapeDtypeStruct((bsz * n, d), F32),
        scratch_shapes=[pltpu.VMEM((n + 2 * CONV_HALO, d), BF16), pltpu.VMEM((nw, d), BF16)],
        compiler_params=_params("arbitrary", "arbitrary"),
        name="conformer_post",
    )(x, amat.astype(BF16), amat.T[CONV_HALO:CONV_HALO + tm].astype(BF16), gtab, b_dw.reshape(1, d),
      ln_g.reshape(1, d), ln_b.reshape(1, d), w, b.reshape(1, d), h, g.reshape(1, d), mod)


def kernel(x, c, ctx, c_ctx, w_mod, b_mod, norm_mix_pre, norm_mix_post, norm_mlp_pre, norm_mlp_post, w_mlp_in, w_mlp_out, attn_w_qkv, attn_w_out, attn_lambda, attn_subln, hy_w_in, hy_b_in, hy_w_short, hy_b_short, hy_filt_w1, hy_filt_b1, hy_filt_w2, hy_filt_b2, hy_filt_freq, hy_filt_w_out, hy_bias, hy_w_out, hy_b_out, cv_w_pw1, cv_b_pw1, cv_w_dw, cv_b_dw, cv_ln_g, cv_ln_b, cv_w_pw2, cv_b_pw2):
    bsz, seq, d = x.shape
    n_ctx = ctx.shape[1]
    depth = w_mod.shape[0]
    assert bsz + 1 <= MOD_ROWS and d % (2 * MXU_DIM) == 0

    cc = jnp.concatenate([c, c_ctx[None, :], jnp.zeros((MOD_ROWS - bsz - 1, d), F32)], axis=0)
    mods = _mod_vectors(cc, w_mod, b_mod)
    rope_tabs = _rope_tables(seq)

    h_lat = x.reshape(bsz * seq, d)
    h_ctx = ctx.reshape(bsz * n_ctx, d)
    lat = dict(rows_per_mod=seq)
    cx = dict(rows_per_mod=bsz * n_ctx)
    for i in range(depth):
        last = i == depth - 1
        kind, j = i % N_MIXERS, i // N_MIXERS
        ctx_out = not last
        mod_l = mods[i, :bsz].reshape(bsz, N_MOD, d)
        mod_c = mods[i, bsz:bsz + 1].reshape(1, N_MOD, d)
        g_pre, g_post = norm_mix_pre[i], norm_mix_post[i]
        mix_l = mix_c = None
        if kind == 0:
            lam_init = 0.8 - 0.6 * math.exp(-0.3 * i)
            w_qkv, w_o = attn_w_qkv[j].astype(BF16), attn_w_out[j].astype(BF16)
            q_l, k_l, vt_l = _pre_project(h_lat, g_pre, mod_l, w_qkv, None, kind="attn", seq=seq,
                                          rope_tabs=rope_tabs, **lat)
            q_c, k_c, vt_c = _pre_project(h_ctx, g_pre, mod_c, w_qkv, None, kind="attn", seq=n_ctx, **cx)
            o_l = _diff_attention(q_l, [(k_c, vt_c), (k_l, vt_l)], attn_lambda[j], attn_subln[j], lam_init)
            mix_l = (o_l, w_o, None, g_post)
            if ctx_out:
                o_c = _diff_attention(q_c, [(k_c, vt_c)], attn_lambda[j], attn_subln[j], lam_init)
                mix_c = (o_c, w_o, None, g_post)
        elif kind == 1:
            w_in, w_o = hy_w_in[j].astype(BF16), hy_w_out[j].astype(BF16)

            def hyena(h, mod, n, rpm):
                x0, vv = _pre_project(h, g_pre, mod, w_in, hy_b_in[j], kind="hyena", rows_per_mod=rpm, seq=n,
                                      short=(hy_w_short[j], hy_b_short[j]))
                hf, hb = _hyena_filters(n, hy_filt_w1[j], hy_filt_b1[j], hy_filt_w2[j], hy_filt_b2[j],
                                        hy_filt_freq[j], hy_filt_w_out[j], d)
                cmat, smat = _dft_matrices(min(n, HYENA_BLOCK))
                kr, ks, kn = _hyena_spectrum(cmat, smat, hf, hb)
                g = _hyena_conv(x0.reshape(bsz, n, d), vv.reshape(bsz, n, d), cmat, smat, kr, ks, kn, hy_bias[j])
                return (g.reshape(bsz * n, d), w_o, hy_b_out[j], g_post)

            mix_l = hyena(h_lat, mod_l, seq, seq)
            if ctx_out:
                mix_c = hyena(h_ctx, mod_c, n_ctx, bsz * n_ctx)
        else:
            w_1, w_2 = cv_w_pw1[j].astype(BF16), cv_w_pw2[j].astype(BF16)

            def conformer(h, mod, n, rpm, per_batch_mod):
                a = _pre_project(h, g_pre, mod, w_1, cv_b_pw1[j], kind="conformer", rows_per_mod=rpm)
                return _conformer_post(a.reshape(bsz, n, d), cv_w_dw[j], cv_b_dw[j], cv_ln_g[j], cv_ln_b[j],
                                       w_2, cv_b_pw2[j], h, g_post, mod, per_batch_mod=per_batch_mod)

            h_lat = conformer(h_lat, mod_l, seq, seq, True)
            if ctx_out:
                h_ctx = conformer(h_ctx, mod_c, n_ctx, bsz * n_ctx, False)
        w1, w2 = w_mlp_in[i].astype(BF16), w_mlp_out[i].astype(BF16)
        h_lat = _mlp_sublayer(h_lat, norm_mlp_pre[i], norm_mlp_post[i], mod_l, w1, w2, mix=mix_l, **lat)
        if ctx_out:
            h_ctx = _mlp_sublayer(h_ctx, norm_mlp_pre[i], norm_mlp_post[i], mod_c, w1, w2, mix=mix_c, **cx)
    return h_lat.reshape(bsz, seq, d)
```

```python
import functools
import math

import jax
import jax.numpy as jnp
from jax import lax
from jax.experimental import pallas as pl
from jax.experimental.pallas import tpu as pltpu

F32 = jnp.float32
BF16 = jnp.bfloat16

N_MIXERS = 3
N_MOD = 6
HEAD_DIM = 64
V_HEAD_DIM = 2 * HEAD_DIM
GRID_W = 64
ROPE_THETA = 10000.0
NORM_EPS = 1e-6
LN_EPS = 1e-5
HYENA_EMB_DIM = 33
HYENA_DECAY_TARGET = 1e-2
HYENA_FAST_DECAY_PCT = 0.3
HYENA_SLOW_DECAY_PCT = 1.5

LANES = 128
SUBLANES = 8
MXU_DIM = 256
VMEM_LIMIT_BYTES = 56 * 1024 * 1024
MOD_ROWS = 24
CONV_HALO = 16
HYENA_BLOCK = 512


def _params(*sem):
    return pltpu.CompilerParams(dimension_semantics=sem, vmem_limit_bytes=VMEM_LIMIT_BYTES)


def _resident(shape, index_map):
    return pl.BlockSpec(shape, index_map, pipeline_mode=pl.Buffered(1))


def _row_tile(rows, want):
    t = min(rows, want)
    assert rows % t == 0
    return t


def _rms(x, g):
    return x * lax.rsqrt(jnp.mean(x * x, axis=-1, keepdims=True) + NORM_EPS) * g


def _dot(a, b):
    return jnp.dot(a, b, preferred_element_type=F32)


def _mod_kernel(c_ref, w_ref, b_ref, o_ref):
    c = c_ref[...]
    s = (c / (1.0 + jnp.exp(-c))).astype(BF16)
    o_ref[0] = _dot(s, w_ref[0].astype(BF16)) + b_ref[0]


def _mod_vectors(cc, w_mod, b_mod):
    depth, d, n = w_mod.shape
    tn = 1024
    return pl.pallas_call(
        _mod_kernel,
        grid=(depth, n // tn),
        in_specs=[pl.BlockSpec((MOD_ROWS, d), lambda i, j: (0, 0)),
                  pl.BlockSpec((1, d, tn), lambda i, j: (i, 0, j)),
                  pl.BlockSpec((1, 1, tn), lambda i, j: (i, 0, j))],
        out_specs=pl.BlockSpec((1, MOD_ROWS, tn), lambda i, j: (i, 0, j)),
        out_shape=jax.ShapeDtypeStruct((depth, MOD_ROWS, n), F32),
        compiler_params=_params("arbitrary", "arbitrary"),
        name="mod_vectors",
    )(cc, w_mod, b_mod.reshape(depth, 1, n))


def _rope_group(z, cos, sin, first_half):
    partner = jnp.where(first_half, pltpu.roll(z, LANES - 16, axis=1), pltpu.roll(z, 16, axis=1))
    return z * cos + partner * sin


def _pre_kernel(*refs, kind, d, rope, per):
    h_ref, g_ref, mod_ref, w_ref = refs[:4]
    rest = list(refs[4:])
    b_ref = rest.pop(0) if kind != "attn" else None
    cos_ref, sin_ref = (rest.pop(0), rest.pop(0)) if rope else (None, None)
    if kind == "attn":
        oq_ref, ok_ref, ovt_ref = rest
    elif kind == "hyena":
        hp_ref, hn_ref, ws_ref, bs_ref, ox0_ref, ovv_ref = rest
    else:
        (o_ref,) = rest

    def modulated(x):
        return _rms(x, g_ref[...]) * (1.0 + mod_ref[0, 1:2, :]) + mod_ref[0, 0:1, :]

    u = modulated(h_ref[...])
    ub = u.astype(BF16)
    nc = 2 * MXU_DIM
    if kind == "hyena":
        tm = u.shape[0]
        i = pl.program_id(0)
        rowx = lax.broadcasted_iota(jnp.int32, (tm + 2 * SUBLANES, 1), 0)
        outside = ((rowx < SUBLANES) & (i % per == 0)) | ((rowx >= tm + SUBLANES) & (i % per == per - 1))
        keep = jnp.where(outside, 0.0, 1.0)
        u_ext = jnp.concatenate([modulated(hp_ref[...]), u, modulated(hn_ref[...])], axis=0).astype(BF16)

        def conv_cols(c0):
            cols = slice(c0, c0 + nc)
            z = (_dot(u_ext, w_ref[:, cols]) + b_ref[:, cols]) * keep
            prev = pltpu.roll(z, 1, axis=0)[SUBLANES:SUBLANES + tm]
            nxt = pltpu.roll(z, tm + 2 * SUBLANES - 1, axis=0)[SUBLANES:SUBLANES + tm]
            return (ws_ref[0:1, cols] * prev + ws_ref[1:2, cols] * z[SUBLANES:SUBLANES + tm]
                    + ws_ref[2:3, cols] * nxt + bs_ref[:, cols])

        for c in range(d // nc):
            ox0_ref[:, c * nc:(c + 1) * nc] = conv_cols(c * nc).astype(ox0_ref.dtype)
            ovv_ref[:, c * nc:(c + 1) * nc] = (conv_cols(2 * d + c * nc) * conv_cols(d + c * nc)).astype(ovv_ref.dtype)
        return
    if kind == "conformer":
        for c in range(d // nc):
            a = _dot(ub, w_ref[:, c * nc:(c + 1) * nc]) + b_ref[:, c * nc:(c + 1) * nc]
            g = _dot(ub, w_ref[:, d + c * nc:d + (c + 1) * nc]) + b_ref[:, d + c * nc:d + (c + 1) * nc]
            o_ref[:, c * nc:(c + 1) * nc] = (a / (1.0 + jnp.exp(-g))).astype(o_ref.dtype)
        return
    if rope:
        lane = lax.broadcasted_iota(jnp.int32, (1, LANES), 1)
        first_half = (lane % 32) < 16
        cos, sin = cos_ref[...], sin_ref[...]
    for c in range(3 * d // nc):
        z = _dot(ub, w_ref[:, c * nc:(c + 1) * nc])
        section = c * nc // d
        for gi in range(nc // LANES):
            head = (c * nc - section * d) // LANES + gi
            zg = z[:, gi * LANES:(gi + 1) * LANES]
            if rope and section < 2:
                zg = _rope_group(zg, cos, sin, first_half)
            if section == 0:
                oq_ref[0, head] = (zg * (HEAD_DIM ** -0.5 * math.log2(math.e))).astype(oq_ref.dtype)
            elif section == 1:
                ok_ref[0, head] = zg.astype(ok_ref.dtype)
            else:
                ovt_ref[0, head] = zg.T.astype(ovt_ref.dtype)


def _pre_project(h, g, mod, w, b, *, kind, rows_per_mod, seq=None, rope_tabs=None, short=None, tm=1024):
    rows, d = h.shape
    n_in = w.shape[1]
    n_out = d if kind == "conformer" else n_in
    tm = _row_tile(min(rows, rows_per_mod, seq or rows), tm)
    rope = rope_tabs is not None
    per = seq // tm if seq else None
    if kind == "attn":
        nh = d // V_HEAD_DIM
        qk_spec = pl.BlockSpec((1, nh, tm, V_HEAD_DIM), lambda i: (i // per, 0, i % per, 0))
        out_specs = [qk_spec, qk_spec, pl.BlockSpec((1, nh, V_HEAD_DIM, tm), lambda i: (i // per, 0, 0, i % per))]
        qk_shape = jax.ShapeDtypeStruct((rows // seq, nh, seq, V_HEAD_DIM), BF16)
        out_shape = [qk_shape, qk_shape, jax.ShapeDtypeStruct((rows // seq, nh, V_HEAD_DIM, seq), BF16)]
    elif kind == "hyena":
        out_specs = [pl.BlockSpec((tm, d), lambda i: (i, 0))] * 2
        out_shape = [jax.ShapeDtypeStruct((rows, d), BF16)] * 2
    else:
        out_specs = pl.BlockSpec((tm, n_out), lambda i: (i, 0))
        out_shape = jax.ShapeDtypeStruct((rows, n_out), BF16)
    ins = [h, g.reshape(1, d), mod, w]
    specs = [pl.BlockSpec((tm, d), lambda i: (i, 0)),
             pl.BlockSpec((1, d), lambda i: (0, 0)),
             pl.BlockSpec((1, N_MOD, d), lambda i: (i * tm // rows_per_mod, 0, 0)),
             _resident((d, n_in), lambda i: (0, 0))]
    if kind != "attn":
        ins.append(b.reshape(1, n_in))
        specs.append(pl.BlockSpec((1, n_in), lambda i: (0, 0)))
    if rope:
        assert rope_tabs[0].shape[0] == seq
        ins += list(rope_tabs)
        specs += [pl.BlockSpec((tm, LANES), lambda i: (i % per, 0))] * 2
    if kind == "hyena":
        w_short, b_short = short
        assert w_short.shape[0] == 3 and tm % SUBLANES == 0
        groups, last = tm // SUBLANES, rows // SUBLANES - 1
        ins += [h, h, w_short, b_short.reshape(1, n_in)]
        specs += [pl.BlockSpec((SUBLANES, d), lambda i: (jnp.maximum(i * groups - 1, 0), 0)),
                  pl.BlockSpec((SUBLANES, d), lambda i: (jnp.minimum((i + 1) * groups, last), 0)),
                  pl.BlockSpec((3, n_in), lambda i: (0, 0)), pl.BlockSpec((1, n_in), lambda i: (0, 0))]
    return pl.pallas_call(
        functools.partial(_pre_kernel, kind=kind, d=d, rope=rope, per=per),
        grid=(rows // tm,),
        in_specs=specs,
        out_specs=out_specs,
        out_shape=out_shape,
        compiler_params=_params("arbitrary"),
        name=f"pre_{kind}",
    )(*ins)


def _rope_tables(seq):
    t = jnp.arange(seq)
    pos = jnp.stack([(t // GRID_W).astype(F32), (t % GRID_W).astype(F32)], axis=1)
    half = HEAD_DIM // 4
    inv_freq = ROPE_THETA ** (-jnp.arange(half, dtype=F32) / half)
    ang = pos[:, :, None] * inv_freq[None, None, :]
    cos = jnp.concatenate([jnp.cos(ang), jnp.cos(ang)], axis=-1).reshape(seq, HEAD_DIM)
    sin = jnp.concatenate([-jnp.sin(ang), jnp.sin(ang)], axis=-1).reshape(seq, HEAD_DIM)
    rep = LANES // HEAD_DIM
    return jnp.tile(cos, (1, rep)), jnp.tile(sin, (1, rep))


def _attn_kernel(*refs, n_kv, lam_init, tq):
    lam_ref, sub_ref, q_ref = refs[:3]
    k_refs = refs[3:3 + 2 * n_kv:2]
    vt_refs = refs[4:4 + 2 * n_kv:2]
    o_ref, st_scr, acc_scr, den_scr = refs[3 + 2 * n_kv:]
    lk_all = st_scr.shape[1]
    qc = st_scr.shape[2] // 2
    kp = min(MXU_DIM, *(k_ref.shape[2] for k_ref in k_refs))
    parts = [(k_ref, vt_ref, o) for k_ref, vt_ref in zip(k_refs, vt_refs) for o in range(0, k_ref.shape[2], kp)]
    n_parts = len(parts)
    assert n_parts * kp == lk_all

    lv = lam_ref[...]
    lam = (jnp.exp(jnp.sum(lv[0:1] * lv[1:2], axis=1, keepdims=True))
           - jnp.exp(jnp.sum(lv[2:3] * lv[3:4], axis=1, keepdims=True)) + lam_init)
    lane = lax.broadcasted_iota(jnp.int32, (1, V_HEAD_DIM), 1)

    def stacked_queries(c):
        q = q_ref[0, 0, pl.ds(pl.multiple_of(c * qc, qc), qc), :]
        zero = jnp.zeros_like(q)
        return jnp.concatenate([jnp.where(lane < HEAD_DIM, q, zero), jnp.where(lane >= HEAD_DIM, q, zero)], axis=0)

    def scores_part(qq, slot, p):
        k_ref, _, off = parts[p]
        st = lax.dot_general(k_ref[0, 0, off:off + kp, :], qq, (((1,), (1,)), ((), ())),
                             preferred_element_type=F32)
        st_scr[slot, p * kp:(p + 1) * kp, :] = st
        return jnp.max(st, axis=0, keepdims=True)

    def scores(c, slot):
        qq = stacked_queries(c)
        return functools.reduce(jnp.maximum, [scores_part(qq, slot, p) for p in range(n_parts)])

    def finish(c, slot):
        acc = acc_scr[slot]
        r = 1.0 / den_scr[slot]
        ot = acc[:, :qc] * r[:, :qc] - acc[:, qc:] * (lam * r[:, qc:])
        o = _rms(ot.T, sub_ref[...]) * (1.0 - lam_init)
        o_ref[0, 0, pl.ds(pl.multiple_of(c * qc, qc), qc), :] = o.astype(o_ref.dtype)

    def chunk(c, slot, m, nxt, prev):
        qq_next = stacked_queries(c + 1) if nxt else None
        m_next = denom = acc = None
        for p in range(n_parts):
            if nxt:
                pm = scores_part(qq_next, 1 - slot, p)
                m_next = pm if m_next is None else jnp.maximum(m_next, pm)
            if prev and p == 0:
                finish(c - 1, 1 - slot)
            e = jnp.exp2(st_scr[slot, p * kp:(p + 1) * kp, :] - m)
            ds = jnp.sum(e, axis=0, keepdims=True)
            _, vt_ref, off = parts[p]
            da = _dot(vt_ref[0, 0, :, off:off + kp], e.astype(BF16))
            denom = ds if denom is None else denom + ds
            acc = da if acc is None else acc + da
        acc_scr[slot] = acc
        den_scr[slot] = denom
        return m_next

    n_chunks = tq // qc
    assert n_chunks == 1 or n_chunks % 2 == 0
    m = scores(0, 0)
    if n_chunks > 1:
        m = chunk(0, 0, m, True, False)

        def pair(i, m):
            return chunk(2 * i + 2, 0, chunk(2 * i + 1, 1, m, True, True), True, True)
        m = lax.fori_loop(0, n_chunks // 2 - 1, pair, m)
        chunk(n_chunks - 1, 1, m, False, True)
        finish(n_chunks - 1, 1)
    else:
        chunk(0, 0, m, False, False)
        finish(0, 0)


def _diff_attention(q, kv_srcs, lam_vec, subln, lam_init, tq=2048, qc=256):
    bsz, n_heads, lq, _ = q.shape
    tq = _row_tile(lq, tq)
    qc = _row_tile(tq, qc)
    lk_all = sum(k.shape[2] for k, _ in kv_srcs)
    ins = [lam_vec, subln.reshape(1, V_HEAD_DIM), q]
    specs = [pl.BlockSpec(lam_vec.shape, lambda b, h, i: (0, 0)),
             pl.BlockSpec((1, V_HEAD_DIM), lambda b, h, i: (0, 0)),
             pl.BlockSpec((1, 1, tq, V_HEAD_DIM), lambda b, h, i: (b, h, i, 0))]
    for k, vt in kv_srcs:
        lk = k.shape[2]
        ins += [k, vt]
        specs += [pl.BlockSpec((1, 1, lk, V_HEAD_DIM), lambda b, h, i: (b, h, 0, 0)),
                  pl.BlockSpec((1, 1, V_HEAD_DIM, lk), lambda b, h, i: (b, h, 0, 0))]
    return pl.pallas_call(
        functools.partial(_attn_kernel, n_kv=len(kv_srcs), lam_init=lam_init, tq=tq),
        grid=(bsz, n_heads, lq // tq),
        in_specs=specs,
        out_specs=pl.BlockSpec((1, 1, tq, V_HEAD_DIM), lambda b, h, i: (b, h, i, 0)),
        out_shape=jax.ShapeDtypeStruct((bsz, n_heads, lq, V_HEAD_DIM), BF16),
        scratch_shapes=[pltpu.VMEM((2, lk_all, 2 * qc), F32), pltpu.VMEM((2, V_HEAD_DIM, 2 * qc), F32),
                        pltpu.VMEM((2, 1, 2 * qc), F32)],
        compiler_params=_params("arbitrary", "arbitrary", "arbitrary"),
        name="diff_attention",
    )(*ins)


def _mlp_kernel(*refs, mix, mix_bias):
    refs = list(refs)
    if mix:
        y_ref, wo_ref = refs.pop(0), refs.pop(0)
        bo_ref = refs.pop(0) if mix_bias else None
        gmix_ref = refs.pop(0)
    h_ref, gpre_ref, gpost_ref, mod_ref, w1_ref, w2_ref, o_ref, hid_ref = refs
    x = h_ref[...]
    if mix:
        if len(y_ref.shape) == 4:
            yin = jnp.concatenate([y_ref[0, hd] for hd in range(y_ref.shape[1])], axis=1)
        else:
            yin = y_ref[...]
        y = _dot(yin, wo_ref[...])
        if mix_bias:
            y = y + bo_ref[...]
        x = x + mod_ref[0, 2:3, :] * _rms(y, gmix_ref[...])
    v = (_rms(x, gpre_ref[...]) * (1.0 + mod_ref[0, 4:5, :]) + mod_ref[0, 3:4, :]).astype(BF16)
    dff = w1_ref.shape[1]
    fc = 4 * MXU_DIM
    for c in range(dff // fc):
        hid = jnp.maximum(_dot(v, w1_ref[:, c * fc:(c + 1) * fc]), 0.0)
        hid_ref[:, c * fc:(c + 1) * fc] = (hid * hid).astype(BF16)
    y = _dot(hid_ref[...], w2_ref[...])
    o_ref[...] = x + mod_ref[0, 5:6, :] * _rms(y, gpost_ref[...])


def _mlp_sublayer(h, gpre, gpost, mod, w1, w2, *, rows_per_mod, mix=None, tm=1024):
    rows, d = h.shape
    dff = w1.shape[1]
    head_major = mix is not None and mix[0].ndim == 4
    tm = _row_tile(min(rows, rows_per_mod, mix[0].shape[2] if head_major else rows), tm)
    tile = pl.BlockSpec((tm, d), lambda i: (i, 0))
    vec = pl.BlockSpec((1, d), lambda i: (0, 0))
    ins, specs, mix_bias = [], [], False
    if mix is not None:
        y, wo, bo, gmix = mix
        mix_bias = bo is not None
        ins = [y, wo] + ([bo.reshape(1, d)] if mix_bias else []) + [gmix.reshape(1, d)]
        if head_major:
            per = y.shape[2] // tm
            y_spec = pl.BlockSpec((1, y.shape[1], tm, V_HEAD_DIM), lambda i: (i // per, 0, i % per, 0))
        else:
            y_spec = tile
        specs = [y_spec, _resident((d, d), lambda i: (0, 0))] + ([vec] if mix_bias else []) + [vec]
    return pl.pallas_call(
        functools.partial(_mlp_kernel, mix=mix is not None, mix_bias=mix_bias),
        grid=(rows // tm,),
        in_specs=specs + [tile, vec, vec,
                          pl.BlockSpec((1, N_MOD, d), lambda i: (i * tm // rows_per_mod, 0, 0)),
                          _resident((d, dff), lambda i: (0, 0)),
                          _resident((dff, d), lambda i: (0, 0))],
        out_specs=tile,
        out_shape=jax.ShapeDtypeStruct((rows, d), F32),
        scratch_shapes=[pltpu.VMEM((tm, dff), BF16)],
        compiler_params=_params("arbitrary"),
        name="mlp_sublayer",
    )(*ins, h, gpre.reshape(1, d), gpost.reshape(1, d), mod, w1, w2)


def _hyena_filter_kernel(z_ref, w1_ref, b1_ref, w2_ref, b2_ref, fr_ref, wo_ref, dl_ref, hf_ref, hb_ref):
    hi = lax.Precision.HIGHEST
    z = z_ref[...]
    freq = fr_ref[...]
    hdn = jnp.sin(freq * (jnp.dot(z, w1_ref[...], precision=hi, preferred_element_type=F32) + b1_ref[...]))
    for j in range(w2_ref.shape[0]):
        hdn = jnp.sin(freq * (jnp.dot(hdn, w2_ref[j], precision=hi, preferred_element_type=F32) + b2_ref[j]))
    h = jnp.dot(hdn, wo_ref[...], precision=hi, preferred_element_type=F32)
    d = hf_ref.shape[1]
    window = jnp.exp(-z[:, 0:1] * dl_ref[...])
    hf_ref[...] = h[:, :d] * window
    hb_ref[...] = h[:, d:] * window


def _hyena_filters(n, w1, b1, w2, b2, freq, w_out, d):
    bands = (HYENA_EMB_DIM - 1) // 2
    t = jnp.linspace(0.0, 1.0, n, dtype=F32)[:, None]
    w = 2.0 * math.pi * jnp.arange(n, dtype=F32)[:, None] / n
    f = jnp.linspace(1e-4, bands - 1, bands, dtype=F32)[None, :]
    z = jnp.concatenate([t, jnp.cos(f * w), -jnp.sin(f * w),
                         jnp.zeros((n, LANES - HYENA_EMB_DIM), F32)], axis=-1)
    w1p = jnp.concatenate([w1, jnp.zeros((LANES - HYENA_EMB_DIM, w1.shape[1]), F32)], axis=0)
    deltas = jnp.abs(jnp.linspace(math.log(HYENA_DECAY_TARGET) / HYENA_SLOW_DECAY_PCT,
                                  math.log(HYENA_DECAY_TARGET) / HYENA_FAST_DECAY_PCT, d, dtype=F32))
    order = w1.shape[1]
    tr = _row_tile(n, 256)
    full = lambda *shape: pl.BlockSpec(shape, lambda i: (0,) * len(shape))
    return pl.pallas_call(
        _hyena_filter_kernel,
        grid=(n // tr,),
        in_specs=[pl.BlockSpec((tr, LANES), lambda i: (i, 0)),
                  full(LANES, order), full(1, order), full(*w2.shape), full(w2.shape[0], 1, order),
                  full(1, order), full(order, 2 * d), full(1, d)],
        out_specs=[pl.BlockSpec((tr, d), lambda i: (i, 0))] * 2,
        out_shape=[jax.ShapeDtypeStruct((n, d), F32)] * 2,
        compiler_params=_params("arbitrary"),
        name="hyena_filters",
    )(z, w1p, b1.reshape(1, order), w2, b2.reshape(w2.shape[0], 1, order), freq.reshape(1, order),
      w_out, deltas.reshape(1, d))


def _dft_matrices(n):
    k = jnp.arange(n, dtype=jnp.int32)
    blk = min(n, LANES)

    def tables(t):
        ang = ((k[:, None] * t[None, :]) % (2 * n)).astype(F32) * (math.pi / n)
        return jnp.cos(ang), jnp.sin(ang)

    ca, sa = tables(jnp.arange(n // blk, dtype=jnp.int32) * blk)
    cb, sb = tables(jnp.arange(blk, dtype=jnp.int32))
    cmat = ca[:, :, None] * cb[:, None, :] - sa[:, :, None] * sb[:, None, :]
    smat = sa[:, :, None] * cb[:, None, :] + ca[:, :, None] * sb[:, None, :]
    return cmat.reshape(n, n).astype(BF16), smat.reshape(n, n).astype(BF16)


def _alt_sign(n):
    row = lax.broadcasted_iota(jnp.int32, (n, 1), 0)
    return jnp.where(row % 2 == 0, 1.0, -1.0).astype(F32), row


def _hyena_spec_kernel(c_ref, s_ref, hf_ref, hb_ref, kr_ref, ks_ref, kn_ref):
    m = c_ref.shape[0]
    n = hf_ref.shape[0]
    nb = n // m
    sgn, row = _alt_sign(m)
    wk = jnp.where(row == 0, 0.5 / m, 1.0 / m)

    def transforms(seg):
        sb = seg.astype(BF16)
        first = sb[0:1, :].astype(F32)
        ct, st = _dot(c_ref[...], sb), _dot(s_ref[...], sb)
        at = jnp.sum(sgn * seg, axis=0, keepdims=True)
        return ct, st, at, ct - first, at - seg[0:1, :]

    fwd = [transforms(hf_ref[j * m:(j + 1) * m, :]) for j in range(nb)]
    bwd = [transforms(hb_ref[j * m:(j + 1) * m, :]) for j in range(nb)]
    for lag in range(-(nb - 1), nb):
        if lag == 0:
            kr, ks, kn = fwd[0][0] + bwd[0][3], fwd[0][1] - bwd[0][1], fwd[0][2] + bwd[0][4]
        else:
            cur, prv = (fwd[lag], fwd[lag - 1]) if lag > 0 else (bwd[-lag], bwd[-lag - 1])
            kr = cur[0] + sgn * prv[3]
            ks = cur[1] + sgn * prv[1]
            kn = cur[2] + prv[4]
            if lag < 0:
                ks = -ks
        kr_ref[lag + nb - 1] = kr * wk
        ks_ref[lag + nb - 1] = ks * wk
        kn_ref[lag + nb - 1] = kn * (0.5 / m)


def _hyena_spectrum(cmat, smat, hf, hb):
    n, d = hf.shape
    m = cmat.shape[0]
    assert n % m == 0 and m % 2 == 0
    nl = 2 * (n // m) - 1
    cb = min(d, MXU_DIM)
    return pl.pallas_call(
        _hyena_spec_kernel,
        grid=(d // cb,),
        in_specs=[_resident((m, m), lambda j: (0, 0)), _resident((m, m), lambda j: (0, 0)),
                  pl.BlockSpec((n, cb), lambda j: (0, j)), pl.BlockSpec((n, cb), lambda j: (0, j))],
        out_specs=[pl.BlockSpec((nl, m, cb), lambda j: (0, 0, j)), pl.BlockSpec((nl, m, cb), lambda j: (0, 0, j)),
                   pl.BlockSpec((nl, 1, cb), lambda j: (0, 0, j))],
        out_shape=[jax.ShapeDtypeStruct((nl, m, d), F32), jax.ShapeDtypeStruct((nl, m, d), F32),
                   jax.ShapeDtypeStruct((nl, 1, d), F32)],
        compiler_params=_params("arbitrary"),
        name="hyena_spectrum",
    )(cmat, smat, hf, hb)


def _hyena_conv_kernel(x0_ref, vv_ref, c_ref, s_ref, kr_ref, ks_ref, kn_ref, db_ref, o_ref,
                       vb_scr, fr_scr, fs_scr, p_scr, q_scr):
    n, cb = o_ref.shape[1], o_ref.shape[2]
    m = c_ref.shape[0]
    nb = n // m
    sgn, _ = _alt_sign(m)

    fn = []
    for jb in range(nb):
        blk = vv_ref[0, jb * m:(jb + 1) * m, :]
        vb_scr[:, jb * cb:(jb + 1) * cb] = blk
        fn.append(jnp.sum(blk.astype(F32) * sgn, axis=0, keepdims=True))
    fr_scr[...] = _dot(c_ref[...], vb_scr[...])
    fs_scr[...] = _dot(s_ref[...], vb_scr[...])

    rc = min(m, 32)
    nyq = []
    for ib in range(nb):
        nyq.append(functools.reduce(jnp.add, [fn[jb] * kn_ref[ib - jb + nb - 1] for jb in range(nb)]))
        for r0 in range(0, m, rc):
            rows = slice(r0, r0 + rc)
            p = q = None
            for jb in range(nb):
                cols = slice(jb * cb, (jb + 1) * cb)
                fr, fs = fr_scr[rows, cols], fs_scr[rows, cols]
                kr, ks = kr_ref[ib - jb + nb - 1, rows, :], ks_ref[ib - jb + nb - 1, rows, :]
                dp, dq = fr * kr - fs * ks, fr * ks + fs * kr
                p, q = (dp, dq) if p is None else (p + dp, q + dq)
            p_scr[rows, ib * cb:(ib + 1) * cb] = p.astype(BF16)
            q_scr[rows, ib * cb:(ib + 1) * cb] = q.astype(BF16)
    y = _dot(c_ref[...], p_scr[...]) + _dot(s_ref[...], q_scr[...])
    for ib in range(nb):
        rows = slice(ib * m, (ib + 1) * m)
        yb = y[:, ib * cb:(ib + 1) * cb] + sgn * nyq[ib] + vv_ref[0, rows, :].astype(F32) * db_ref[...]
        o_ref[0, rows, :] = (yb * x0_ref[0, rows, :].astype(F32)).astype(o_ref.dtype)


def _hyena_conv(x0, vv, cmat, smat, kr, ks, kn, d_bias):
    bsz, n, d = vv.shape
    cb = min(d, MXU_DIM)
    nb = d // cb
    m, nl = cmat.shape[0], kr.shape[0]
    wide = (m, (n // m) * cb)
    seq = pl.BlockSpec((1, n, cb), lambda j, b: (b, 0, j))
    spec = pl.BlockSpec((nl, m, cb), lambda j, b: (0, 0, j))
    nyq = pl.BlockSpec((nl, 1, cb), lambda j, b: (0, 0, j))
    vec = pl.BlockSpec((1, cb), lambda j, b: (0, j))
    return pl.pallas_call(
        _hyena_conv_kernel,
        grid=(nb, bsz),
        in_specs=[seq, seq, _resident((m, m), lambda j, b: (0, 0)), _resident((m, m), lambda j, b: (0, 0)),
                  spec, spec, nyq, vec],
        out_specs=seq,
        out_shape=jax.ShapeDtypeStruct((bsz, n, d), BF16),
        scratch_shapes=[pltpu.VMEM(wide, BF16), pltpu.VMEM(wide, F32), pltpu.VMEM(wide, F32),
                        pltpu.VMEM(wide, BF16), pltpu.VMEM(wide, BF16)],
        compiler_params=_params("arbitrary", "arbitrary"),
        name="hyena_conv",
    )(x0, vv, cmat, smat, kr, ks, kn, d_bias.reshape(1, d))


def _window_dft(nw):
    half = nw // 2
    t = jnp.arange(nw, dtype=jnp.int32)
    ang = ((jnp.arange(half, dtype=jnp.int32)[:, None] * t[None, :]) % nw).astype(F32) * (2.0 * math.pi / nw)
    nyq = jnp.where(t % 2 == 0, 1.0, -1.0).astype(F32)[None, :]
    return jnp.concatenate([jnp.cos(ang), nyq, jnp.sin(ang)[1:]], axis=0)


def _tap_spectrum_kernel(tr_ref, ts_ref, td_ref, w_ref, g_ref):
    hi = lax.Precision.HIGHEST
    for i, t_ref in enumerate((tr_ref, ts_ref, td_ref)):
        g_ref[i] = jnp.dot(t_ref[...], w_ref[...], precision=hi, preferred_element_type=F32)


def _tap_spectrum(w_dw, nw):
    width, d = w_dw.shape
    reach, half = width // 2, nw // 2
    kpad = -width % SUBLANES
    lag = reach - jnp.arange(width + kpad, dtype=jnp.int32)
    ang = ((jnp.arange(half, dtype=jnp.int32)[:, None] * lag[None, :]) % nw).astype(F32) * (2.0 * math.pi / nw)
    wf = jnp.where(jnp.arange(half) == 0, 1.0 / nw, 2.0 / nw).astype(F32)[:, None]
    tr = jnp.cos(ang) * wf
    ts = jnp.sin(ang) * wf
    nyq = jnp.where(lag % 2 == 0, 1.0 / nw, -1.0 / nw).astype(F32)[None, :]
    td = jnp.concatenate([nyq, tr[1:]], axis=0)
    wp = jnp.concatenate([w_dw, jnp.zeros((kpad, d), F32)], axis=0)
    full = lambda a: pl.BlockSpec(a.shape, lambda: (0,) * a.ndim)
    return pl.pallas_call(
        _tap_spectrum_kernel,
        in_specs=[full(tr), full(ts), full(td), full(wp)],
        out_specs=pl.BlockSpec((3, half, d), lambda: (0, 0, 0)),
        out_shape=jax.ShapeDtypeStruct((3, half, d), F32),
        name="tap_spectrum",
    )(tr, ts, td, wp)


def _conformer_post_kernel(x_ref, a_ref, ai_ref, gt_ref, bd_ref, lg_ref, lb_ref, w_ref, b_ref, h_ref, g_ref,
                           mod_ref, o_ref, xp_scr, ys_scr):
    n, d = x_ref.shape[1], x_ref.shape[2]
    nw = a_ref.shape[0]
    half = nw // 2
    tm = o_ref.shape[0]
    j = pl.program_id(1)

    @pl.when(j == 0)
    def _():
        xp_scr[0:CONV_HALO, :] = jnp.zeros((CONV_HALO, d), BF16)
        xp_scr[CONV_HALO:CONV_HALO + n, :] = x_ref[0]
        xp_scr[CONV_HALO + n:, :] = jnp.zeros((CONV_HALO, d), BF16)

    base = pl.multiple_of(j * tm, tm)
    f = _dot(a_ref[...], xp_scr[pl.ds(base, nw), :])
    xr, xs = f[:half], f[half:]
    ys_scr[0:half, :] = (xr * gt_ref[0] - xs * gt_ref[1]).astype(BF16)
    ys_scr[half:, :] = (xr * gt_ref[1] + xs * gt_ref[2]).astype(BF16)
    z = _dot(ai_ref[...], ys_scr[...]) + bd_ref[...]
    mu = jnp.mean(z, axis=-1, keepdims=True)
    zc = z - mu
    var = jnp.mean(zc * zc, axis=-1, keepdims=True)
    zn = zc * lax.rsqrt(var + LN_EPS) * lg_ref[...] + lb_ref[...]
    act = (zn / (1.0 + jnp.exp(-zn))).astype(BF16)
    y = _dot(act, w_ref[...]) + b_ref[...]
    o_ref[...] = h_ref[...] + mod_ref[0, 2:3, :] * _rms(y, g_ref[...])


def _conformer_post(x, w_dw, b_dw, ln_g, ln_b, w, b, h, g, mod, *, per_batch_mod, tm=256):
    bsz, n, d = x.shape
    assert w_dw.shape[0] // 2 < CONV_HALO
    tm = _row_tile(n, tm)
    per = n // tm
    nw = tm + 2 * CONV_HALO
    amat = _window_dft(nw)
    gtab = _tap_spectrum(w_dw, nw)
    vec = pl.BlockSpec((1, d), lambda bi, j: (0, 0))
    return pl.pallas_call(
        _conformer_post_kernel,
        grid=(bsz, per),
        in_specs=[pl.BlockSpec((1, n, d), lambda bi, j: (bi, 0, 0)),
                  _resident((nw, nw), lambda bi, j: (0, 0)), _resident((tm, nw), lambda bi, j: (0, 0)),
                  _resident((3, nw // 2, d), lambda bi, j: (0, 0, 0)), vec, vec, vec,
                  _resident((d, d), lambda bi, j: (0, 0)), vec,
                  pl.BlockSpec((tm, d), lambda bi, j: (bi * per + j, 0)), vec,
                  pl.BlockSpec((1, N_MOD, d), lambda bi, j: (bi if per_batch_mod else 0, 0, 0))],
        out_specs=pl.BlockSpec((tm, d), lambda bi, j: (bi * per + j, 0)),
        out_shape=jax.ShapeDtypeStruct((bsz * n, d), F32),
        scratch_shapes=[pltpu.VMEM((n + 2 * CONV_HALO, d), BF16), pltpu.VMEM((nw, d), BF16)],
        compiler_params=_params("arbitrary", "arbitrary"),
        name="conformer_post",
    )(x, amat.astype(BF16), amat.T[CONV_HALO:CONV_HALO + tm].astype(BF16), gtab, b_dw.reshape(1, d),
      ln_g.reshape(1, d), ln_b.reshape(1, d), w, b.reshape(1, d), h, g.reshape(1, d), mod)


def kernel(x, c, ctx, c_ctx, w_mod, b_mod, norm_mix_pre, norm_mix_post, norm_mlp_pre, norm_mlp_post, w_mlp_in, w_mlp_out, attn_w_qkv, attn_w_out, attn_lambda, attn_subln, hy_w_in, hy_b_in, hy_w_short, hy_b_short, hy_filt_w1, hy_filt_b1, hy_filt_w2, hy_filt_b2, hy_filt_freq, hy_filt_w_out, hy_bias, hy_w_out, hy_b_out, cv_w_pw1, cv_b_pw1, cv_w_dw, cv_b_dw, cv_ln_g, cv_ln_b, cv_w_pw2, cv_b_pw2):
    bsz, seq, d = x.shape
    n_ctx = ctx.shape[1]
    depth = w_mod.shape[0]
    assert bsz + 1 <= MOD_ROWS and d % (2 * MXU_DIM) == 0

    cc = jnp.concatenate([c, c_ctx[None, :], jnp.zeros((MOD_ROWS - bsz - 1, d), F32)], axis=0)
    mods = _mod_vectors(cc, w_mod, b_mod)
    rope_tabs = _rope_tables(seq)

    h_lat = x.reshape(bsz * seq, d)
    h_ctx = ctx.reshape(bsz * n_ctx, d)
    lat = dict(rows_per_mod=seq)
    cx = dict(rows_per_mod=bsz * n_ctx)
    for i in range(depth):
        last = i == depth - 1
        kind, j = i % N_MIXERS, i // N_MIXERS
        ctx_out = not last
        mod_l = mods[i, :bsz].reshape(bsz, N_MOD, d)
        mod_c = mods[i, bsz:bsz + 1].reshape(1, N_MOD, d)
        g_pre, g_post = norm_mix_pre[i], norm_mix_post[i]
        mix_l = mix_c = None
        if kind == 0:
            lam_init = 0.8 - 0.6 * math.exp(-0.3 * i)
            w_qkv, w_o = attn_w_qkv[j].astype(BF16), attn_w_out[j].astype(BF16)
            q_l, k_l, vt_l = _pre_project(h_lat, g_pre, mod_l, w_qkv, None, kind="attn", seq=seq,
                                          rope_tabs=rope_tabs, **lat)
            q_c, k_c, vt_c = _pre_project(h_ctx, g_pre, mod_c, w_qkv, None, kind="attn", seq=n_ctx, **cx)
            o_l = _diff_attention(q_l, [(k_c, vt_c), (k_l, vt_l)], attn_lambda[j], attn_subln[j], lam_init)
            mix_l = (o_l, w_o, None, g_post)
            if ctx_out:
                o_c = _diff_attention(q_c, [(k_c, vt_c)], attn_lambda[j], attn_subln[j], lam_init)
                mix_c = (o_c, w_o, None, g_post)
        elif kind == 1:
            w_in, w_o = hy_w_in[j].astype(BF16), hy_w_out[j].astype(BF16)

            def hyena(h, mod, n, rpm):
                x0, vv = _pre_project(h, g_pre, mod, w_in, hy_b_in[j], kind="hyena", rows_per_mod=rpm, seq=n,
                                      short=(hy_w_short[j], hy_b_short[j]))
                hf, hb = _hyena_filters(n, hy_filt_w1[j], hy_filt_b1[j], hy_filt_w2[j], hy_filt_b2[j],
                                        hy_filt_freq[j], hy_filt_w_out[j], d)
                cmat, smat = _dft_matrices(min(n, HYENA_BLOCK))
                kr, ks, kn = _hyena_spectrum(cmat, smat, hf, hb)
                g = _hyena_conv(x0.reshape(bsz, n, d), vv.reshape(bsz, n, d), cmat, smat, kr, ks, kn, hy_bias[j])
                return (g.reshape(bsz * n, d), w_o, hy_b_out[j], g_post)

            mix_l = hyena(h_lat, mod_l, seq, seq)
            if ctx_out:
                mix_c = hyena(h_ctx, mod_c, n_ctx, bsz * n_ctx)
        else:
            w_1, w_2 = cv_w_pw1[j].astype(BF16), cv_w_pw2[j].astype(BF16)

            def conformer(h, mod, n, rpm, per_batch_mod):
                a = _pre_project(h, g_pre, mod, w_1, cv_b_pw1[j], kind="conformer", rows_per_mod=rpm)
                return _conformer_post(a.reshape(bsz, n, d), cv_w_dw[j], cv_b_dw[j], cv_ln_g[j], cv_ln_b[j],
                                       w_2, cv_b_pw2[j], h, g_post, mod, per_batch_mod=per_batch_mod)

            h_lat = conformer(h_lat, mod_l, seq, seq, True)
            if ctx_out:
                h_ctx = conformer(h_ctx, mod_c, n_ctx, bsz * n_ctx, False)
        w1, w2 = w_mlp_in[i].astype(BF16), w_mlp_out[i].astype(BF16)
        h_lat = _mlp_sublayer(h_lat, norm_mlp_pre[i], norm_mlp_post[i], mod_l, w1, w2, mix=mix_l, **lat)
        if ctx_out:
            h_ctx = _mlp_sublayer(h_ctx, norm_mlp_pre[i], norm_mlp_post[i], mod_c, w1, w2, mix=mix_c, **cx)
    return h_lat.reshape(bsz, seq, d)
```

```python
import functools
import math

import jax
import jax.numpy as jnp
from jax import lax
from jax.experimental import pallas as pl
from jax.experimental.pallas import tpu as pltpu

F32 = jnp.float32
BF16 = jnp.bfloat16

N_MIXERS = 3
N_MOD = 6
HEAD_DIM = 64
V_HEAD_DIM = 2 * HEAD_DIM
GRID_W = 64
ROPE_THETA = 10000.0
NORM_EPS = 1e-6
LN_EPS = 1e-5
HYENA_EMB_DIM = 33
HYENA_DECAY_TARGET = 1e-2
HYENA_FAST_DECAY_PCT = 0.3
HYENA_SLOW_DECAY_PCT = 1.5

LANES = 128
SUBLANES = 8
MXU_DIM = 256
VMEM_LIMIT_BYTES = 56 * 1024 * 1024
MOD_ROWS = 24
CONV_HALO = 16
HYENA_BLOCK = 512


def _params(*sem):
    return pltpu.CompilerParams(dimension_semantics=sem, vmem_limit_bytes=VMEM_LIMIT_BYTES)


def _resident(shape, index_map):
    return pl.BlockSpec(shape, index_map, pipeline_mode=pl.Buffered(1))


def _row_tile(rows, want):
    t = min(rows, want)
    assert rows % t == 0
    return t


def _rms(x, g):
    return x * lax.rsqrt(jnp.mean(x * x, axis=-1, keepdims=True) + NORM_EPS) * g


def _dot(a, b):
    return jnp.dot(a, b, preferred_element_type=F32)


def _mod_kernel(c_ref, w_ref, b_ref, o_ref):
    c = c_ref[...]
    s = (c / (1.0 + jnp.exp(-c))).astype(BF16)
    o_ref[0] = _dot(s, w_ref[0].astype(BF16)) + b_ref[0]


def _mod_vectors(cc, w_mod, b_mod):
    depth, d, n = w_mod.shape
    tn = 1024
    return pl.pallas_call(
        _mod_kernel,
        grid=(depth, n // tn),
        in_specs=[pl.BlockSpec((MOD_ROWS, d), lambda i, j: (0, 0)),
                  pl.BlockSpec((1, d, tn), lambda i, j: (i, 0, j)),
                  pl.BlockSpec((1, 1, tn), lambda i, j: (i, 0, j))],
        out_specs=pl.BlockSpec((1, MOD_ROWS, tn), lambda i, j: (i, 0, j)),
        out_shape=jax.ShapeDtypeStruct((depth, MOD_ROWS, n), F32),
        compiler_params=_params("arbitrary", "arbitrary"),
        name="mod_vectors",
    )(cc, w_mod, b_mod.reshape(depth, 1, n))


def _rope_group(z, cos, sin, first_half):
    partner = jnp.where(first_half, pltpu.roll(z, LANES - 16, axis=1), pltpu.roll(z, 16, axis=1))
    return z * cos + partner * sin


def _pre_kernel(*refs, kind, d, rope, per):
    h_ref, g_ref, mod_ref, w_ref = refs[:4]
    rest = list(refs[4:])
    b_ref = rest.pop(0) if kind != "attn" else None
    cos_ref, sin_ref = (rest.pop(0), rest.pop(0)) if rope else (None, None)
    if kind == "attn":
        oq_ref, ok_ref, ovt_ref = rest
    elif kind == "hyena":
        hp_ref, hn_ref, ws_ref, bs_ref, ox0_ref, ovv_ref = rest
    else:
        (o_ref,) = rest

    def modulated(x):
        return _rms(x, g_ref[...]) * (1.0 + mod_ref[0, 1:2, :]) + mod_ref[0, 0:1, :]

    u = modulated(h_ref[...])
    ub = u.astype(BF16)
    nc = 2 * MXU_DIM
    if kind == "hyena":
        tm = u.shape[0]
        i = pl.program_id(0)
        rowx = lax.broadcasted_iota(jnp.int32, (tm + 2 * SUBLANES, 1), 0)
        outside = ((rowx < SUBLANES) & (i % per == 0)) | ((rowx >= tm + SUBLANES) & (i % per == per - 1))
        keep = jnp.where(outside, 0.0, 1.0)
        u_ext = jnp.concatenate([modulated(hp_ref[...]), u, modulated(hn_ref[...])], axis=0).astype(BF16)

        def conv_cols(c0):
            cols = slice(c0, c0 + nc)
            z = (_dot(u_ext, w_ref[:, cols]) + b_ref[:, cols]) * keep
            prev = pltpu.roll(z, 1, axis=0)[SUBLANES:SUBLANES + tm]
            nxt = pltpu.roll(z, tm + 2 * SUBLANES - 1, axis=0)[SUBLANES:SUBLANES + tm]
            return (ws_ref[0:1, cols] * prev + ws_ref[1:2, cols] * z[SUBLANES:SUBLANES + tm]
                    + ws_ref[2:3, cols] * nxt + bs_ref[:, cols])

        for c in range(d // nc):
            ox0_ref[:, c * nc:(c + 1) * nc] = conv_cols(c * nc).astype(ox0_ref.dtype)
            ovv_ref[:, c * nc:(c + 1) * nc] = (conv_cols(2 * d + c * nc) * conv_cols(d + c * nc)).astype(ovv_ref.dtype)
        return
    if kind == "conformer":
        for c in range(d // nc):
            a = _dot(ub, w_ref[:, c * nc:(c + 1) * nc]) + b_ref[:, c * nc:(c + 1) * nc]
            g = _dot(ub, w_ref[:, d + c * nc:d + (c + 1) * nc]) + b_ref[:, d + c * nc:d + (c + 1) * nc]
            o_ref[:, c * nc:(c + 1) * nc] = (a / (1.0 + jnp.exp(-g))).astype(o_ref.dtype)
        return
    if rope:
        lane = lax.broadcasted_iota(jnp.int32, (1, LANES), 1)
        first_half = (lane % 32) < 16
        cos, sin = cos_ref[...], sin_ref[...]
    for c in range(3 * d // nc):
        z = _dot(ub, w_ref[:, c * nc:(c + 1) * nc])
        section = c * nc // d
        for gi in range(nc // LANES):
            head = (c * nc - section * d) // LANES + gi
            zg = z[:, gi * LANES:(gi + 1) * LANES]
            if rope and section < 2:
                zg = _rope_group(zg, cos, sin, first_half)
            if section == 0:
                oq_ref[0, head] = (zg * (HEAD_DIM ** -0.5 * math.log2(math.e))).astype(oq_ref.dtype)
            elif section == 1:
                ok_ref[0, head] = zg.astype(ok_ref.dtype)
            else:
                ovt_ref[0, head] = zg.T.astype(ovt_ref.dtype)


def _pre_project(h, g, mod, w, b, *, kind, rows_per_mod, seq=None, rope_tabs=None, short=None, tm=1024):
    rows, d = h.shape
    n_in = w.shape[1]
    n_out = d if kind == "conformer" else n_in
    tm = _row_tile(min(rows, rows_per_mod, seq or rows), tm)
    rope = rope_tabs is not None
    per = seq // tm if seq else None
    if kind == "attn":
        nh = d // V_HEAD_DIM
        qk_spec = pl.BlockSpec((1, nh, tm, V_HEAD_DIM), lambda i: (i // per, 0, i % per, 0))
        out_specs = [qk_spec, qk_spec, pl.BlockSpec((1, nh, V_HEAD_DIM, tm), lambda i: (i // per, 0, 0, i % per))]
        qk_shape = jax.ShapeDtypeStruct((rows // seq, nh, seq, V_HEAD_DIM), BF16)
        out_shape = [qk_shape, qk_shape, jax.ShapeDtypeStruct((rows // seq, nh, V_HEAD_DIM, seq), BF16)]
    elif kind == "hyena":
        out_specs = [pl.BlockSpec((tm, d), lambda i: (i, 0))] * 2
        out_shape = [jax.ShapeDtypeStruct((rows, d), BF16)] * 2
    else:
        out_specs = pl.BlockSpec((tm, n_out), lambda i: (i, 0))
        out_shape = jax.ShapeDtypeStruct((rows, n_out), BF16)
    ins = [h, g.reshape(1, d), mod, w]
    specs = [pl.BlockSpec((tm, d), lambda i: (i, 0)),
             pl.BlockSpec((1, d), lambda i: (0, 0)),
             pl.BlockSpec((1, N_MOD, d), lambda i: (i * tm // rows_per_mod, 0, 0)),
             _resident((d, n_in), lambda i: (0, 0))]
    if kind != "attn":
        ins.append(b.reshape(1, n_in))
        specs.append(pl.BlockSpec((1, n_in), lambda i: (0, 0)))
    if rope:
        assert rope_tabs[0].shape[0] == seq
        ins += list(rope_tabs)
        specs += [pl.BlockSpec((tm, LANES), lambda i: (i % per, 0))] * 2
    if kind == "hyena":
        w_short, b_short = short
        assert w_short.shape[0] == 3 and tm % SUBLANES == 0
        groups, last = tm // SUBLANES, rows // SUBLANES - 1
        ins += [h, h, w_short, b_short.reshape(1, n_in)]
        specs += [pl.BlockSpec((SUBLANES, d), lambda i: (jnp.maximum(i * groups - 1, 0), 0)),
                  pl.BlockSpec((SUBLANES, d), lambda i: (jnp.minimum((i + 1) * groups, last), 0)),
                  pl.BlockSpec((3, n_in), lambda i: (0, 0)), pl.BlockSpec((1, n_in), lambda i: (0, 0))]
    return pl.pallas_call(
        functools.partial(_pre_kernel, kind=kind, d=d, rope=rope, per=per),
        grid=(rows // tm,),
        in_specs=specs,
        out_specs=out_specs,
        out_shape=out_shape,
        compiler_params=_params("arbitrary"),
        name=f"pre_{kind}",
    )(*ins)


def _rope_tables(seq):
    t = jnp.arange(seq)
    pos = jnp.stack([(t // GRID_W).astype(F32), (t % GRID_W).astype(F32)], axis=1)
    half = HEAD_DIM // 4
    inv_freq = ROPE_THETA ** (-jnp.arange(half, dtype=F32) / half)
    ang = pos[:, :, None] * inv_freq[None, None, :]
    cos = jnp.concatenate([jnp.cos(ang), jnp.cos(ang)], axis=-1).reshape(seq, HEAD_DIM)
    sin = jnp.concatenate([-jnp.sin(ang), jnp.sin(ang)], axis=-1).reshape(seq, HEAD_DIM)
    rep = LANES // HEAD_DIM
    return jnp.tile(cos, (1, rep)), jnp.tile(sin, (1, rep))


def _attn_kernel(*refs, n_kv, lam_init, tq):
    lam_ref, sub_ref, q_ref = refs[:3]
    k_refs = refs[3:3 + 2 * n_kv:2]
    vt_refs = refs[4:4 + 2 * n_kv:2]
    o_ref, st_scr, acc_scr, den_scr = refs[3 + 2 * n_kv:]
    lk_all = st_scr.shape[1]
    qc = st_scr.shape[2] // 2
    kp = min(MXU_DIM, *(k_ref.shape[2] for k_ref in k_refs))
    parts = [(k_ref, vt_ref, o) for k_ref, vt_ref in zip(k_refs, vt_refs) for o in range(0, k_ref.shape[2], kp)]
    n_parts = len(parts)
    assert n_parts * kp == lk_all

    lv = lam_ref[...]
    lam = (jnp.exp(jnp.sum(lv[0:1] * lv[1:2], axis=1, keepdims=True))
           - jnp.exp(jnp.sum(lv[2:3] * lv[3:4], axis=1, keepdims=True)) + lam_init)
    lane = lax.broadcasted_iota(jnp.int32, (1, V_HEAD_DIM), 1)

    def stacked_queries(c):
        q = q_ref[0, 0, pl.ds(pl.multiple_of(c * qc, qc), qc), :]
        zero = jnp.zeros_like(q)
        return jnp.concatenate([jnp.where(lane < HEAD_DIM, q, zero), jnp.where(lane >= HEAD_DIM, q, zero)], axis=0)

    def scores_part(qq, slot, p):
        k_ref, _, off = parts[p]
        st = lax.dot_general(k_ref[0, 0, off:off + kp, :], qq, (((1,), (1,)), ((), ())),
                             preferred_element_type=F32)
        st_scr[slot, p * kp:(p + 1) * kp, :] = st
        return jnp.max(st, axis=0, keepdims=True)

    def scores(c, slot):
        qq = stacked_queries(c)
        return functools.reduce(jnp.maximum, [scores_part(qq, slot, p) for p in range(n_parts)])

    def finish(c, slot):
        acc = acc_scr[slot]
        r = 1.0 / den_scr[slot]
        ot = acc[:, :qc] * r[:, :qc] - acc[:, qc:] * (lam * r[:, qc:])
        o = _rms(ot.T, sub_ref[...]) * (1.0 - lam_init)
        o_ref[0, 0, pl.ds(pl.multiple_of(c * qc, qc), qc), :] = o.astype(o_ref.dtype)

    def chunk(c, slot, m, nxt, prev):
        qq_next = stacked_queries(c + 1) if nxt else None
        m_next = denom = acc = None
        for p in range(n_parts):
            if nxt:
                pm = scores_part(qq_next, 1 - slot, p)
                m_next = pm if m_next is None else jnp.maximum(m_next, pm)
            if prev and p == 0:
                finish(c - 1, 1 - slot)
            e = jnp.exp2(st_scr[slot, p * kp:(p + 1) * kp, :] - m)
            ds = jnp.sum(e, axis=0, keepdims=True)
            _, vt_ref, off = parts[p]
            da = _dot(vt_ref[0, 0, :, off:off + kp], e.astype(BF16))
            denom = ds if denom is None else denom + ds
            acc = da if acc is None else acc + da
        acc_scr[slot] = acc
        den_scr[slot] = denom
        return m_next

    n_chunks = tq // qc
    assert n_chunks == 1 or n_chunks % 2 == 0
    m = scores(0, 0)
    if n_chunks > 1:
        m = chunk(0, 0, m, True, False)

        def pair(i, m):
            return chunk(2 * i + 2, 0, chunk(2 * i + 1, 1, m, True, True), True, True)
        m = lax.fori_loop(0, n_chunks // 2 - 1, pair, m)
        chunk(n_chunks - 1, 1, m, False, True)
        finish(n_chunks - 1, 1)
    else:
        chunk(0, 0, m, False, False)
        finish(0, 0)


def _diff_attention(q, kv_srcs, lam_vec, subln, lam_init, tq=2048, qc=256):
    bsz, n_heads, lq, _ = q.shape
    tq = _row_tile(lq, tq)
    qc = _row_tile(tq, qc)
    lk_all = sum(k.shape[2] for k, _ in kv_srcs)
    ins = [lam_vec, subln.reshape(1, V_HEAD_DIM), q]
    specs = [pl.BlockSpec(lam_vec.shape, lambda b, h, i: (0, 0)),
             pl.BlockSpec((1, V_HEAD_DIM), lambda b, h, i: (0, 0)),
             pl.BlockSpec((1, 1, tq, V_HEAD_DIM), lambda b, h, i: (b, h, i, 0))]
    for k, vt in kv_srcs:
        lk = k.shape[2]
        ins += [k, vt]
        specs += [pl.BlockSpec((1, 1, lk, V_HEAD_DIM), lambda b, h, i: (b, h, 0, 0)),
                  pl.BlockSpec((1, 1, V_HEAD_DIM, lk), lambda b, h, i: (b, h, 0, 0))]
    return pl.pallas_call(
        functools.partial(_attn_kernel, n_kv=len(kv_srcs), lam_init=lam_init, tq=tq),
        grid=(bsz, n_heads, lq // tq),
        in_specs=specs,
        out_specs=pl.BlockSpec((1, 1, tq, V_HEAD_DIM), lambda b, h, i: (b, h, i, 0)),
        out_shape=jax.ShapeDtypeStruct((bsz, n_heads, lq, V_HEAD_DIM), BF16),
        scratch_shapes=[pltpu.VMEM((2, lk_all, 2 * qc), F32), pltpu.VMEM((2, V_HEAD_DIM, 2 * qc), F32),
                        pltpu.VMEM((2, 1, 2 * qc), F32)],
        compiler_params=_params("arbitrary", "arbitrary", "arbitrary"),
        name="diff_attention",
    )(*ins)


def _mlp_kernel(*refs, mix, mix_bias):
    refs = list(refs)
    if mix:
        y_ref, wo_ref = refs.pop(0), refs.pop(0)
        bo_ref = refs.pop(0) if mix_bias else None
        gmix_ref = refs.pop(0)
    h_ref, gpre_ref, gpost_ref, mod_ref, w1_ref, w2_ref, o_ref, hid_ref = refs
    x = h_ref[...]
    if mix:
        if len(y_ref.shape) == 4:
            yin = jnp.concatenate([y_ref[0, hd] for hd in range(y_ref.shape[1])], axis=1)
        else:
            yin = y_ref[...]
        y = _dot(yin, wo_ref[...])
        if mix_bias:
            y = y + bo_ref[...]
        x = x + mod_ref[0, 2:3, :] * _rms(y, gmix_ref[...])
    v = (_rms(x, gpre_ref[...]) * (1.0 + mod_ref[0, 4:5, :]) + mod_ref[0, 3:4, :]).astype(BF16)
    dff = w1_ref.shape[1]
    fc = 4 * MXU_DIM
    for c in range(dff // fc):
        hid = jnp.maximum(_dot(v, w1_ref[:, c * fc:(c + 1) * fc]), 0.0)
        hid_ref[:, c * fc:(c + 1) * fc] = (hid * hid).astype(BF16)
    y = _dot(hid_ref[...], w2_ref[...])
    o_ref[...] = x + mod_ref[0, 5:6, :] * _rms(y, gpost_ref[...])


def _mlp_sublayer(h, gpre, gpost, mod, w1, w2, *, rows_per_mod, mix=None, tm=1024):
    rows, d = h.shape
    dff = w1.shape[1]
    head_major = mix is not None and mix[0].ndim == 4
    tm = _row_tile(min(rows, rows_per_mod, mix[0].shape[2] if head_major else rows), tm)
    tile = pl.BlockSpec((tm, d), lambda i: (i, 0))
    vec = pl.BlockSpec((1, d), lambda i: (0, 0))
    ins, specs, mix_bias = [], [], False
    if mix is not None:
        y, wo, bo, gmix = mix
        mix_bias = bo is not None
        ins = [y, wo] + ([bo.reshape(1, d)] if mix_bias else []) + [gmix.reshape(1, d)]
        if head_major:
            per = y.shape[2] // tm
            y_spec = pl.BlockSpec((1, y.shape[1], tm, V_HEAD_DIM), lambda i: (i // per, 0, i % per, 0))
        else:
            y_spec = tile
        specs = [y_spec, _resident((d, d), lambda i: (0, 0))] + ([vec] if mix_bias else []) + [vec]
    return pl.pallas_call(
        functools.partial(_mlp_kernel, mix=mix is not None, mix_bias=mix_bias),
        grid=(rows // tm,),
        in_specs=specs + [tile, vec, vec,
                          pl.BlockSpec((1, N_MOD, d), lambda i: (i * tm // rows_per_mod, 0, 0)),
                          _resident((d, dff), lambda i: (0, 0)),
                          _resident((dff, d), lambda i: (0, 0))],
        out_specs=tile,
        out_shape=jax.ShapeDtypeStruct((rows, d), F32),
        scratch_shapes=[pltpu.VMEM((tm, dff), BF16)],
        compiler_params=_params("arbitrary"),
        name="mlp_sublayer",
    )(*ins, h, gpre.reshape(1, d), gpost.reshape(1, d), mod, w1, w2)


def _hyena_filter_kernel(z_ref, w1_ref, b1_ref, w2_ref, b2_ref, fr_ref, wo_ref, dl_ref, hf_ref, hb_ref):
    hi = lax.Precision.HIGHEST
    z = z_ref[...]
    freq = fr_ref[...]
    hdn = jnp.sin(freq * (jnp.dot(z, w1_ref[...], precision=hi, preferred_element_type=F32) + b1_ref[...]))
    for j in range(w2_ref.shape[0]):
        hdn = jnp.sin(freq * (jnp.dot(hdn, w2_ref[j], precision=hi, preferred_element_type=F32) + b2_ref[j]))
    h = jnp.dot(hdn, wo_ref[...], precision=hi, preferred_element_type=F32)
    d = hf_ref.shape[1]
    window = jnp.exp(-z[:, 0:1] * dl_ref[...])
    hf_ref[...] = h[:, :d] * window
    hb_ref[...] = h[:, d:] * window


def _hyena_filters(n, w1, b1, w2, b2, freq, w_out, d):
    bands = (HYENA_EMB_DIM - 1) // 2
    t = jnp.linspace(0.0, 1.0, n, dtype=F32)[:, None]
    w = 2.0 * math.pi * jnp.arange(n, dtype=F32)[:, None] / n
    f = jnp.linspace(1e-4, bands - 1, bands, dtype=F32)[None, :]
    z = jnp.concatenate([t, jnp.cos(f * w), -jnp.sin(f * w),
                         jnp.zeros((n, LANES - HYENA_EMB_DIM), F32)], axis=-1)
    w1p = jnp.concatenate([w1, jnp.zeros((LANES - HYENA_EMB_DIM, w1.shape[1]), F32)], axis=0)
    deltas = jnp.abs(jnp.linspace(math.log(HYENA_DECAY_TARGET) / HYENA_SLOW_DECAY_PCT,
                                  math.log(HYENA_DECAY_TARGET) / HYENA_FAST_DECAY_PCT, d, dtype=F32))
    order = w1.shape[1]
    tr = _row_tile(n, 256)
    full = lambda *shape: pl.BlockSpec(shape, lambda i: (0,) * len(shape))
    return pl.pallas_call(
        _hyena_filter_kernel,
        grid=(n // tr,),
        in_specs=[pl.BlockSpec((tr, LANES), lambda i: (i, 0)),
                  full(LANES, order), full(1, order), full(*w2.shape), full(w2.shape[0], 1, order),
                  full(1, order), full(order, 2 * d), full(1, d)],
        out_specs=[pl.BlockSpec((tr, d), lambda i: (i, 0))] * 2,
        out_shape=[jax.ShapeDtypeStruct((n, d), F32)] * 2,
        compiler_params=_params("arbitrary"),
        name="hyena_filters",
    )(z, w1p, b1.reshape(1, order), w2, b2.reshape(w2.shape[0], 1, order), freq.reshape(1, order),
      w_out, deltas.reshape(1, d))


def _dft_matrices(n):
    k = jnp.arange(n, dtype=jnp.int32)
    blk = min(n, LANES)

    def tables(t):
        ang = ((k[:, None] * t[None, :]) % (2 * n)).astype(F32) * (math.pi / n)
        return jnp.cos(ang), jnp.sin(ang)

    ca, sa = tables(jnp.arange(n // blk, dtype=jnp.int32) * blk)
    cb, sb = tables(jnp.arange(blk, dtype=jnp.int32))
    cmat = ca[:, :, None] * cb[:, None, :] - sa[:, :, None] * sb[:, None, :]
    smat = sa[:, :, None] * cb[:, None, :] + ca[:, :, None] * sb[:, None, :]
    return cmat.reshape(n, n).astype(BF16), smat.reshape(n, n).astype(BF16)


def _alt_sign(n):
    row = lax.broadcasted_iota(jnp.int32, (n, 1), 0)
    return jnp.where(row % 2 == 0, 1.0, -1.0).astype(F32), row


def _hyena_spec_kernel(c_ref, s_ref, hf_ref, hb_ref, kr_ref, ks_ref, kn_ref):
    m = c_ref.shape[0]
    n = hf_ref.shape[0]
    nb = n // m
    sgn, row = _alt_sign(m)
    wk = jnp.where(row == 0, 0.5 / m, 1.0 / m)

    def transforms(seg):
        sb = seg.astype(BF16)
        first = sb[0:1, :].astype(F32)
        ct, st = _dot(c_ref[...], sb), _dot(s_ref[...], sb)
        at = jnp.sum(sgn * seg, axis=0, keepdims=True)
        return ct, st, at, ct - first, at - seg[0:1, :]

    fwd = [transforms(hf_ref[j * m:(j + 1) * m, :]) for j in range(nb)]
    bwd = [transforms(hb_ref[j * m:(j + 1) * m, :]) for j in range(nb)]
    for lag in range(-(nb - 1), nb):
        if lag == 0:
            kr, ks, kn = fwd[0][0] + bwd[0][3], fwd[0][1] - bwd[0][1], fwd[0][2] + bwd[0][4]
        else:
            cur, prv = (fwd[lag], fwd[lag - 1]) if lag > 0 else (bwd[-lag], bwd[-lag - 1])
            kr = cur[0] + sgn * prv[3]
            ks = cur[1] + sgn * prv[1]
            kn = cur[2] + prv[4]
            if lag < 0:
                ks = -ks
        kr_ref[lag + nb - 1] = kr * wk
        ks_ref[lag + nb - 1] = ks * wk
        kn_ref[lag + nb - 1] = kn * (0.5 / m)


def _hyena_spectrum(cmat, smat, hf, hb):
    n, d = hf.shape
    m = cmat.shape[0]
    assert n % m == 0 and m % 2 == 0
    nl = 2 * (n // m) - 1
    cb = min(d, MXU_DIM)
    return pl.pallas_call(
        _hyena_spec_kernel,
        grid=(d // cb,),
        in_specs=[_resident((m, m), lambda j: (0, 0)), _resident((m, m), lambda j: (0, 0)),
                  pl.BlockSpec((n, cb), lambda j: (0, j)), pl.BlockSpec((n, cb), lambda j: (0, j))],
        out_specs=[pl.BlockSpec((nl, m, cb), lambda j: (0, 0, j)), pl.BlockSpec((nl, m, cb), lambda j: (0, 0, j)),
                   pl.BlockSpec((nl, 1, cb), lambda j: (0, 0, j))],
        out_shape=[jax.ShapeDtypeStruct((nl, m, d), F32), jax.ShapeDtypeStruct((nl, m, d), F32),
                   jax.ShapeDtypeStruct((nl, 1, d), F32)],
        compiler_params=_params("arbitrary"),
        name="hyena_spectrum",
    )(cmat, smat, hf, hb)


def _hyena_conv_kernel(x0_ref, vv_ref, c_ref, s_ref, kr_ref, ks_ref, kn_ref, db_ref, o_ref,
                       vb_scr, fr_scr, fs_scr, p_scr, q_scr):
    n, cb = o_ref.shape[1], o_ref.shape[2]
    m = c_ref.shape[0]
    nb = n // m
    sgn, _ = _alt_sign(m)

    def transform(s):
        fn = []
        for jb in range(nb):
            blk = vv_ref[s, jb * m:(jb + 1) * m, :]
            vb_scr[s, :, jb * cb:(jb + 1) * cb] = blk
            fn.append(jnp.sum(blk.astype(F32) * sgn, axis=0, keepdims=True))
        fr_scr[s] = _dot(c_ref[...], vb_scr[s])
        fs_scr[s] = _dot(s_ref[...], vb_scr[s])
        return fn

    rc = min(m, 32)

    def spectral(s, fn):
        nyq = []
        for ib in range(nb):
            nyq.append(functools.reduce(jnp.add, [fn[jb] * kn_ref[ib - jb + nb - 1] for jb in range(nb)]))
            for r0 in range(0, m, rc):
                rows = slice(r0, r0 + rc)
                p = q = None
                for jb in range(nb):
                    cols = slice(jb * cb, (jb + 1) * cb)
                    fr, fs = fr_scr[s, rows, cols], fs_scr[s, rows, cols]
                    kr, ks = kr_ref[ib - jb + nb - 1, rows, :], ks_ref[ib - jb + nb - 1, rows, :]
                    dp, dq = fr * kr - fs * ks, fr * ks + fs * kr
                    p, q = (dp, dq) if p is None else (p + dp, q + dq)
                p_scr[s, rows, ib * cb:(ib + 1) * cb] = p.astype(BF16)
                q_scr[s, rows, ib * cb:(ib + 1) * cb] = q.astype(BF16)
        return nyq

    def inverse(s):
        return _dot(c_ref[...], p_scr[s]) + _dot(s_ref[...], q_scr[s])

    def emit(s, y, nyq):
        for ib in range(nb):
            rows = slice(ib * m, (ib + 1) * m)
            yb = y[:, ib * cb:(ib + 1) * cb] + sgn * nyq[ib] + vv_ref[s, rows, :].astype(F32) * db_ref[...]
            o_ref[s, rows, :] = (yb * x0_ref[s, rows, :].astype(F32)).astype(o_ref.dtype)

    seqs = range(o_ref.shape[0])
    fn = [transform(s) for s in seqs]
    y, nyq = [], []
    for s in seqs:
        nyq.append(spectral(s, fn[s]))
        y.append(inverse(s))
    for s in seqs:
        emit(s, y[s], nyq[s])


def _hyena_conv(x0, vv, cmat, smat, kr, ks, kn, d_bias):
    bsz, n, d = vv.shape
    cb = min(d, MXU_DIM)
    nb = d // cb
    m, nl = cmat.shape[0], kr.shape[0]
    ns = 1
    wide = (ns, m, (n // m) * cb)
    seq = pl.BlockSpec((ns, n, cb), lambda j, b: (b, 0, j))
    spec = pl.BlockSpec((nl, m, cb), lambda j, b: (0, 0, j))
    nyq = pl.BlockSpec((nl, 1, cb), lambda j, b: (0, 0, j))
    vec = pl.BlockSpec((1, cb), lambda j, b: (0, j))
    return pl.pallas_call(
        _hyena_conv_kernel,
        grid=(nb, bsz // ns),
        in_specs=[seq, seq, _resident((m, m), lambda j, b: (0, 0)), _resident((m, m), lambda j, b: (0, 0)),
                  spec, spec, nyq, vec],
        out_specs=seq,
        out_shape=jax.ShapeDtypeStruct((bsz, n, d), BF16),
        scratch_shapes=[pltpu.VMEM(wide, BF16), pltpu.VMEM(wide, F32), pltpu.VMEM(wide, F32),
                        pltpu.VMEM(wide, BF16), pltpu.VMEM(wide, BF16)],
        compiler_params=_params("arbitrary", "arbitrary"),
        name="hyena_conv",
    )(x0, vv, cmat, smat, kr, ks, kn, d_bias.reshape(1, d))


def _window_dft(nw):
    half = nw // 2
    t = jnp.arange(nw, dtype=jnp.int32)
    ang = ((jnp.arange(half, dtype=jnp.int32)[:, None] * t[None, :]) % nw).astype(F32) * (2.0 * math.pi / nw)
    nyq = jnp.where(t % 2 == 0, 1.0, -1.0).astype(F32)[None, :]
    return jnp.concatenate([jnp.cos(ang), nyq, jnp.sin(ang)[1:]], axis=0)


def _tap_spectrum_kernel(tr_ref, ts_ref, td_ref, w_ref, g_ref):
    hi = lax.Precision.HIGHEST
    for i, t_ref in enumerate((tr_ref, ts_ref, td_ref)):
        g_ref[i] = jnp.dot(t_ref[...], w_ref[...], precision=hi, preferred_element_type=F32)


def _tap_spectrum(w_dw, nw):
    width, d = w_dw.shape
    reach, half = width // 2, nw // 2
    kpad = -width % SUBLANES
    lag = reach - jnp.arange(width + kpad, dtype=jnp.int32)
    ang = ((jnp.arange(half, dtype=jnp.int32)[:, None] * lag[None, :]) % nw).astype(F32) * (2.0 * math.pi / nw)
    wf = jnp.where(jnp.arange(half) == 0, 1.0 / nw, 2.0 / nw).astype(F32)[:, None]
    tr = jnp.cos(ang) * wf
    ts = jnp.sin(ang) * wf
    nyq = jnp.where(lag % 2 == 0, 1.0 / nw, -1.0 / nw).astype(F32)[None, :]
    td = jnp.concatenate([nyq, tr[1:]], axis=0)
    wp = jnp.concatenate([w_dw, jnp.zeros((kpad, d), F32)], axis=0)
    full = lambda a: pl.BlockSpec(a.shape, lambda: (0,) * a.ndim)
    return pl.pallas_call(
        _tap_spectrum_kernel,
        in_specs=[full(tr), full(ts), full(td), full(wp)],
        out_specs=pl.BlockSpec((3, half, d), lambda: (0, 0, 0)),
        out_shape=jax.ShapeDtypeStruct((3, half, d), F32),
        name="tap_spectrum",
    )(tr, ts, td, wp)


def _conformer_post_kernel(x_ref, a_ref, ai_ref, gt_ref, bd_ref, lg_ref, lb_ref, w_ref, b_ref, h_ref, g_ref,
                           mod_ref, o_ref, xp_scr, ys_scr):
    n, d = x_ref.shape[1], x_ref.shape[2]
    nw = a_ref.shape[0]
    half = nw // 2
    tm = o_ref.shape[0]
    j = pl.program_id(1)

    @pl.when(j == 0)
    def _():
        xp_scr[0:CONV_HALO, :] = jnp.zeros((CONV_HALO, d), BF16)
        xp_scr[CONV_HALO:CONV_HALO + n, :] = x_ref[0]
        xp_scr[CONV_HALO + n:, :] = jnp.zeros((CONV_HALO, d), BF16)

    tw = nw - 2 * CONV_HALO
    base = pl.multiple_of(j * tm, tm)

    def forward(w):
        return _dot(a_ref[...], xp_scr[pl.ds(base + w * tw, nw), :])

    def spectral(w, f):
        xr, xs = f[:half], f[half:]
        ys_scr[w, 0:half, :] = (xr * gt_ref[0] - xs * gt_ref[1]).astype(BF16)
        ys_scr[w, half:, :] = (xr * gt_ref[1] + xs * gt_ref[2]).astype(BF16)

    def inverse(w):
        return _dot(ai_ref[...], ys_scr[w]) + bd_ref[...]

    def activate(z):
        mu = jnp.mean(z, axis=-1, keepdims=True)
        zc = z - mu
        var = jnp.mean(zc * zc, axis=-1, keepdims=True)
        zn = zc * lax.rsqrt(var + LN_EPS) * lg_ref[...] + lb_ref[...]
        return (zn / (1.0 + jnp.exp(-zn))).astype(BF16)

    def project(act):
        return _dot(act, w_ref[...]) + b_ref[...]

    def store(w, y):
        rows = slice(w * tw, (w + 1) * tw)
        o_ref[rows, :] = h_ref[rows, :] + mod_ref[0, 2:3, :] * _rms(y, g_ref[...])

    wins = range(tm // tw)
    f = [forward(w) for w in wins]
    z = []
    for w in wins:
        spectral(w, f[w])
        z.append(inverse(w))
    y = []
    for w in wins:
        y.append(project(activate(z[w])))
    for w in wins:
        store(w, y[w])


def _conformer_post(x, w_dw, b_dw, ln_g, ln_b, w, b, h, g, mod, *, per_batch_mod, tm=1024, tw=MXU_DIM):
    bsz, n, d = x.shape
    assert w_dw.shape[0] // 2 < CONV_HALO
    tm = _row_tile(n, tm)
    tw = _row_tile(tm, tw)
    per = n // tm
    nw = tw + 2 * CONV_HALO
    amat = _window_dft(nw)
    gtab = _tap_spectrum(w_dw, nw)
    vec = pl.BlockSpec((1, d), lambda bi, j: (0, 0))
    return pl.pallas_call(
        _conformer_post_kernel,
        grid=(bsz, per),
        in_specs=[pl.BlockSpec((1, n, d), lambda bi, j: (bi, 0, 0)),
                  _resident((nw, nw), lambda bi, j: (0, 0)), _resident((tw, nw), lambda bi, j: (0, 0)),
                  _resident((3, nw // 2, d), lambda bi, j: (0, 0, 0)), vec, vec, vec,
                  _resident((d, d), lambda bi, j: (0, 0)), vec,
                  pl.BlockSpec((tm, d), lambda bi, j: (bi * per + j, 0)), vec,
                  pl.BlockSpec((1, N_MOD, d), lambda bi, j: (bi if per_batch_mod else 0, 0, 0))],
        out_specs=pl.BlockSpec((tm, d), lambda bi, j: (bi * per + j, 0)),
        out_shape=jax.ShapeDtypeStruct((bsz * n, d), F32),
        scratch_shapes=[pltpu.VMEM((n + 2 * CONV_HALO, d), BF16), pltpu.VMEM((tm // tw, nw, d), BF16)],
        compiler_params=_params("arbitrary", "arbitrary"),
        name="conformer_post",
    )(x, amat.astype(BF16), amat.T[CONV_HALO:CONV_HALO + tw].astype(BF16), gtab, b_dw.reshape(1, d),
      ln_g.reshape(1, d), ln_b.reshape(1, d), w, b.reshape(1, d), h, g.reshape(1, d), mod)


def kernel(x, c, ctx, c_ctx, w_mod, b_mod, norm_mix_pre, norm_mix_post, norm_mlp_pre, norm_mlp_post, w_mlp_in, w_mlp_out, attn_w_qkv, attn_w_out, attn_lambda, attn_subln, hy_w_in, hy_b_in, hy_w_short, hy_b_short, hy_filt_w1, hy_filt_b1, hy_filt_w2, hy_filt_b2, hy_filt_freq, hy_filt_w_out, hy_bias, hy_w_out, hy_b_out, cv_w_pw1, cv_b_pw1, cv_w_dw, cv_b_dw, cv_ln_g, cv_ln_b, cv_w_pw2, cv_b_pw2):
    bsz, seq, d = x.shape
    n_ctx = ctx.shape[1]
    depth = w_mod.shape[0]
    assert bsz + 1 <= MOD_ROWS and d % (2 * MXU_DIM) == 0

    cc = jnp.concatenate([c, c_ctx[None, :], jnp.zeros((MOD_ROWS - bsz - 1, d), F32)], axis=0)
    mods = _mod_vectors(cc, w_mod, b_mod)
    rope_tabs = _rope_tables(seq)

    h_lat = x.reshape(bsz * seq, d)
    h_ctx = ctx.reshape(bsz * n_ctx, d)
    lat = dict(rows_per_mod=seq)
    cx = dict(rows_per_mod=bsz * n_ctx)
    for i in range(depth):
        last = i == depth - 1
        kind, j = i % N_MIXERS, i // N_MIXERS
        ctx_out = not last
        mod_l = mods[i, :bsz].reshape(bsz, N_MOD, d)
        mod_c = mods[i, bsz:bsz + 1].reshape(1, N_MOD, d)
        g_pre, g_post = norm_mix_pre[i], norm_mix_post[i]
        mix_l = mix_c = None
        if kind == 0:
            lam_init = 0.8 - 0.6 * math.exp(-0.3 * i)
            w_qkv, w_o = attn_w_qkv[j].astype(BF16), attn_w_out[j].astype(BF16)
            q_l, k_l, vt_l = _pre_project(h_lat, g_pre, mod_l, w_qkv, None, kind="attn", seq=seq,
                                          rope_tabs=rope_tabs, **lat)
            q_c, k_c, vt_c = _pre_project(h_ctx, g_pre, mod_c, w_qkv, None, kind="attn", seq=n_ctx, **cx)
            o_l = _diff_attention(q_l, [(k_c, vt_c), (k_l, vt_l)], attn_lambda[j], attn_subln[j], lam_init)
            mix_l = (o_l, w_o, None, g_post)
            if ctx_out:
                o_c = _diff_attention(q_c, [(k_c, vt_c)], attn_lambda[j], attn_subln[j], lam_init)
                mix_c = (o_c, w_o, None, g_post)
        elif kind == 1:
            w_in, w_o = hy_w_in[j].astype(BF16), hy_w_out[j].astype(BF16)

            def hyena(h, mod, n, rpm):
                x0, vv = _pre_project(h, g_pre, mod, w_in, hy_b_in[j], kind="hyena", rows_per_mod=rpm, seq=n,
                                      short=(hy_w_short[j], hy_b_short[j]))
                hf, hb = _hyena_filters(n, hy_filt_w1[j], hy_filt_b1[j], hy_filt_w2[j], hy_filt_b2[j],
                                        hy_filt_freq[j], hy_filt_w_out[j], d)
                cmat, smat = _dft_matrices(min(n, HYENA_BLOCK))
                kr, ks, kn = _hyena_spectrum(cmat, smat, hf, hb)
                g = _hyena_conv(x0.reshape(bsz, n, d), vv.reshape(bsz, n, d), cmat, smat, kr, ks, kn, hy_bias[j])
                return (g.reshape(bsz * n, d), w_o, hy_b_out[j], g_post)

            mix_l = hyena(h_lat, mod_l, seq, seq)
            if ctx_out:
                mix_c = hyena(h_ctx, mod_c, n_ctx, bsz * n_ctx)
        else:
            w_1, w_2 = cv_w_pw1[j].astype(BF16), cv_w_pw2[j].astype(BF16)

            def conformer(h, mod, n, rpm, per_batch_mod):
                a = _pre_project(h, g_pre, mod, w_1, cv_b_pw1[j], kind="conformer", rows_per_mod=rpm)
                return _conformer_post(a.reshape(bsz, n, d), cv_w_dw[j], cv_b_dw[j], cv_ln_g[j], cv_ln_b[j],
                                       w_2, cv_b_pw2[j], h, g_post, mod, per_batch_mod=per_batch_mod)

            h_lat = conformer(h_lat, mod_l, seq, seq, True)
            if ctx_out:
                h_ctx = conformer(h_ctx, mod_c, n_ctx, bsz * n_ctx, False)
        w1, w2 = w_mlp_in[i].astype(BF16), w_mlp_out[i].astype(BF16)
        h_lat = _mlp_sublayer(h_lat, norm_mlp_pre[i], norm_mlp_post[i], mod_l, w1, w2, mix=mix_l, **lat)
        if ctx_out:
            h_ctx = _mlp_sublayer(h_ctx, norm_mlp_pre[i], norm_mlp_post[i], mod_c, w1, w2, mix=mix_c, **cx)
    return h_lat.reshape(bsz, seq, d)
```

```python
import functools
import math

import jax
import jax.numpy as jnp
from jax import lax
from jax.experimental import pallas as pl
from jax.experimental.pallas import tpu as pltpu

F32 = jnp.float32
BF16 = jnp.bfloat16

N_MIXERS = 3
N_MOD = 6
HEAD_DIM = 64
V_HEAD_DIM = 2 * HEAD_DIM
GRID_W = 64
ROPE_THETA = 10000.0
NORM_EPS = 1e-6
LN_EPS = 1e-5
HYENA_EMB_DIM = 33
HYENA_DECAY_TARGET = 1e-2
HYENA_FAST_DECAY_PCT = 0.3
HYENA_SLOW_DECAY_PCT = 1.5

LANES = 128
SUBLANES = 8
MXU_DIM = 256
VMEM_LIMIT_BYTES = 56 * 1024 * 1024
MOD_ROWS = 24
CONV_HALO = 16
HYENA_BLOCK = 512


def _params(*sem):
    return pltpu.CompilerParams(dimension_semantics=sem, vmem_limit_bytes=VMEM_LIMIT_BYTES)


def _resident(shape, index_map):
    return pl.BlockSpec(shape, index_map, pipeline_mode=pl.Buffered(1))


def _row_tile(rows, want):
    t = min(rows, want)
    assert rows % t == 0
    return t


def _rms(x, g):
    return x * lax.rsqrt(jnp.mean(x * x, axis=-1, keepdims=True) + NORM_EPS) * g


def _dot(a, b):
    return jnp.dot(a, b, preferred_element_type=F32)


def _mod_kernel(c_ref, w_ref, b_ref, o_ref):
    c = c_ref[...]
    s = (c / (1.0 + jnp.exp(-c))).astype(BF16)
    o_ref[0] = _dot(s, w_ref[0].astype(BF16)) + b_ref[0]


def _mod_vectors(cc, w_mod, b_mod):
    depth, d, n = w_mod.shape
    tn = 1024
    return pl.pallas_call(
        _mod_kernel,
        grid=(depth, n // tn),
        in_specs=[pl.BlockSpec((MOD_ROWS, d), lambda i, j: (0, 0)),
                  pl.BlockSpec((1, d, tn), lambda i, j: (i, 0, j)),
                  pl.BlockSpec((1, 1, tn), lambda i, j: (i, 0, j))],
        out_specs=pl.BlockSpec((1, MOD_ROWS, tn), lambda i, j: (i, 0, j)),
        out_shape=jax.ShapeDtypeStruct((depth, MOD_ROWS, n), F32),
        compiler_params=_params("arbitrary", "arbitrary"),
        name="mod_vectors",
    )(cc, w_mod, b_mod.reshape(depth, 1, n))


def _rope_group(z, cos, sin, first_half):
    partner = jnp.where(first_half, pltpu.roll(z, LANES - 16, axis=1), pltpu.roll(z, 16, axis=1))
    return z * cos + partner * sin


def _pre_kernel(*refs, kind, d, rope, per):
    h_ref, g_ref, mod_ref, w_ref = refs[:4]
    rest = list(refs[4:])
    b_ref = rest.pop(0) if kind != "attn" else None
    cos_ref, sin_ref = (rest.pop(0), rest.pop(0)) if rope else (None, None)
    if kind == "attn":
        oq_ref, ok_ref, ovt_ref = rest
    elif kind == "hyena":
        hp_ref, hn_ref, ws_ref, bs_ref, ox0_ref, ovv_ref = rest
    else:
        (o_ref,) = rest

    def modulated(x):
        return _rms(x, g_ref[...]) * (1.0 + mod_ref[0, 1:2, :]) + mod_ref[0, 0:1, :]

    u = modulated(h_ref[...])
    ub = u.astype(BF16)
    nc = 2 * MXU_DIM
    if kind == "hyena":
        tm = u.shape[0]
        i = pl.program_id(0)
        rowx = lax.broadcasted_iota(jnp.int32, (tm + 2 * SUBLANES, 1), 0)
        outside = ((rowx < SUBLANES) & (i % per == 0)) | ((rowx >= tm + SUBLANES) & (i % per == per - 1))
        keep = jnp.where(outside, 0.0, 1.0)
        u_ext = jnp.concatenate([modulated(hp_ref[...]), u, modulated(hn_ref[...])], axis=0).astype(BF16)

        def conv_cols(c0):
            cols = slice(c0, c0 + nc)
            z = (_dot(u_ext, w_ref[:, cols]) + b_ref[:, cols]) * keep
            prev = pltpu.roll(z, 1, axis=0)[SUBLANES:SUBLANES + tm]
            nxt = pltpu.roll(z, tm + 2 * SUBLANES - 1, axis=0)[SUBLANES:SUBLANES + tm]
            return (ws_ref[0:1, cols] * prev + ws_ref[1:2, cols] * z[SUBLANES:SUBLANES + tm]
                    + ws_ref[2:3, cols] * nxt + bs_ref[:, cols])

        for c in range(d // nc):
            ox0_ref[:, c * nc:(c + 1) * nc] = conv_cols(c * nc).astype(ox0_ref.dtype)
            ovv_ref[:, c * nc:(c + 1) * nc] = (conv_cols(2 * d + c * nc) * conv_cols(d + c * nc)).astype(ovv_ref.dtype)
        return
    if kind == "conformer":
        for c in range(d // nc):
            a = _dot(ub, w_ref[:, c * nc:(c + 1) * nc]) + b_ref[:, c * nc:(c + 1) * nc]
            g = _dot(ub, w_ref[:, d + c * nc:d + (c + 1) * nc]) + b_ref[:, d + c * nc:d + (c + 1) * nc]
            o_ref[:, c * nc:(c + 1) * nc] = (a / (1.0 + jnp.exp(-g))).astype(o_ref.dtype)
        return
    if rope:
        lane = lax.broadcasted_iota(jnp.int32, (1, LANES), 1)
        first_half = (lane % 32) < 16
        cos, sin = cos_ref[...], sin_ref[...]
    for c in range(3 * d // nc):
        z = _dot(ub, w_ref[:, c * nc:(c + 1) * nc])
        section = c * nc // d
        for gi in range(nc // LANES):
            head = (c * nc - section * d) // LANES + gi
            zg = z[:, gi * LANES:(gi + 1) * LANES]
            if rope and section < 2:
                zg = _rope_group(zg, cos, sin, first_half)
            if section == 0:
                oq_ref[0, head] = (zg * (HEAD_DIM ** -0.5 * math.log2(math.e))).astype(oq_ref.dtype)
            elif section == 1:
                ok_ref[0, head] = zg.astype(ok_ref.dtype)
            else:
                ovt_ref[0, head] = zg.T.astype(ovt_ref.dtype)


def _pre_project(h, g, mod, w, b, *, kind, rows_per_mod, seq=None, rope_tabs=None, short=None, tm=1024):
    rows, d = h.shape
    n_in = w.shape[1]
    n_out = d if kind == "conformer" else n_in
    tm = _row_tile(min(rows, rows_per_mod, seq or rows), tm)
    rope = rope_tabs is not None
    per = seq // tm if seq else None
    if kind == "attn":
        nh = d // V_HEAD_DIM
        qk_spec = pl.BlockSpec((1, nh, tm, V_HEAD_DIM), lambda i: (i // per, 0, i % per, 0))
        out_specs = [qk_spec, qk_spec, pl.BlockSpec((1, nh, V_HEAD_DIM, tm), lambda i: (i // per, 0, 0, i % per))]
        qk_shape = jax.ShapeDtypeStruct((rows // seq, nh, seq, V_HEAD_DIM), BF16)
        out_shape = [qk_shape, qk_shape, jax.ShapeDtypeStruct((rows // seq, nh, V_HEAD_DIM, seq), BF16)]
    elif kind == "hyena":
        out_specs = [pl.BlockSpec((tm, d), lambda i: (i, 0))] * 2
        out_shape = [jax.ShapeDtypeStruct((rows, d), BF16)] * 2
    else:
        out_specs = pl.BlockSpec((tm, n_out), lambda i: (i, 0))
        out_shape = jax.ShapeDtypeStruct((rows, n_out), BF16)
    ins = [h, g.reshape(1, d), mod, w]
    specs = [pl.BlockSpec((tm, d), lambda i: (i, 0)),
             pl.BlockSpec((1, d), lambda i: (0, 0)),
             pl.BlockSpec((1, N_MOD, d), lambda i: (i * tm // rows_per_mod, 0, 0)),
             _resident((d, n_in), lambda i: (0, 0))]
    if kind != "attn":
        ins.append(b.reshape(1, n_in))
        specs.append(pl.BlockSpec((1, n_in), lambda i: (0, 0)))
    if rope:
        assert rope_tabs[0].shape[0] == seq
        ins += list(rope_tabs)
        specs += [pl.BlockSpec((tm, LANES), lambda i: (i % per, 0))] * 2
    if kind == "hyena":
        w_short, b_short = short
        assert w_short.shape[0] == 3 and tm % SUBLANES == 0
        groups, last = tm // SUBLANES, rows // SUBLANES - 1
        ins += [h, h, w_short, b_short.reshape(1, n_in)]
        specs += [pl.BlockSpec((SUBLANES, d), lambda i: (jnp.maximum(i * groups - 1, 0), 0)),
                  pl.BlockSpec((SUBLANES, d), lambda i: (jnp.minimum((i + 1) * groups, last), 0)),
                  pl.BlockSpec((3, n_in), lambda i: (0, 0)), pl.BlockSpec((1, n_in), lambda i: (0, 0))]
    return pl.pallas_call(
        functools.partial(_pre_kernel, kind=kind, d=d, rope=rope, per=per),
        grid=(rows // tm,),
        in_specs=specs,
        out_specs=out_specs,
        out_shape=out_shape,
        compiler_params=_params("arbitrary"),
        name=f"pre_{kind}",
    )(*ins)


def _rope_tables(seq):
    t = jnp.arange(seq)
    pos = jnp.stack([(t // GRID_W).astype(F32), (t % GRID_W).astype(F32)], axis=1)
    half = HEAD_DIM // 4
    inv_freq = ROPE_THETA ** (-jnp.arange(half, dtype=F32) / half)
    ang = pos[:, :, None] * inv_freq[None, None, :]
    cos = jnp.concatenate([jnp.cos(ang), jnp.cos(ang)], axis=-1).reshape(seq, HEAD_DIM)
    sin = jnp.concatenate([-jnp.sin(ang), jnp.sin(ang)], axis=-1).reshape(seq, HEAD_DIM)
    rep = LANES // HEAD_DIM
    return jnp.tile(cos, (1, rep)), jnp.tile(sin, (1, rep))


def _attn_kernel(*refs, n_kv, lam_init, tq, cross):
    lam_ref, sub_ref, q_ref = refs[:3]
    k_refs = refs[3:3 + 2 * n_kv:2]
    vt_refs = refs[4:4 + 2 * n_kv:2]
    rest = list(refs[3 + 2 * n_kv:])
    qn_ref, kn_refs = (rest.pop(0), [rest.pop(0) for _ in range(n_kv)]) if cross else (None, None)
    o_ref, st_scr, acc_scr, den_scr, m_scr = rest
    lk_all = st_scr.shape[1]
    qc = st_scr.shape[2] // 2
    kp = min(MXU_DIM, *(k_ref.shape[2] for k_ref in k_refs))
    parts = [(k_ref, vt_ref, o) for k_ref, vt_ref in zip(k_refs, vt_refs) for o in range(0, k_ref.shape[2], kp)]
    parts_next = [(kn_ref, None, o) for kn_ref in kn_refs for o in range(0, kn_ref.shape[2], kp)] if cross else None
    n_parts = len(parts)
    assert n_parts * kp == lk_all

    lv = lam_ref[...]
    lam = (jnp.exp(jnp.sum(lv[0:1] * lv[1:2], axis=1, keepdims=True))
           - jnp.exp(jnp.sum(lv[2:3] * lv[3:4], axis=1, keepdims=True)) + lam_init)
    lane = lax.broadcasted_iota(jnp.int32, (1, V_HEAD_DIM), 1)

    def stacked(q):
        zero = jnp.zeros_like(q)
        return jnp.concatenate([jnp.where(lane < HEAD_DIM, q, zero), jnp.where(lane >= HEAD_DIM, q, zero)], axis=0)

    def stacked_queries(c):
        return stacked(q_ref[0, 0, pl.ds(pl.multiple_of(c * qc, qc), qc), :])

    def scores_part(qq, slot, p, key_parts):
        k_ref, _, off = key_parts[p]
        st = lax.dot_general(k_ref[0, 0, off:off + kp, :], qq, (((1,), (1,)), ((), ())),
                             preferred_element_type=F32)
        st_scr[slot, p * kp:(p + 1) * kp, :] = st
        return jnp.max(st, axis=0, keepdims=True)

    def scores(c, slot):
        qq = stacked_queries(c)
        return functools.reduce(jnp.maximum, [scores_part(qq, slot, p, parts) for p in range(n_parts)])

    def finish(c, slot):
        acc = acc_scr[slot]
        r = 1.0 / den_scr[slot]
        ot = acc[:, :qc] * r[:, :qc] - acc[:, qc:] * (lam * r[:, qc:])
        o = _rms(ot.T, sub_ref[...]) * (1.0 - lam_init)
        o_ref[0, 0, pl.ds(pl.multiple_of(c * qc, qc), qc), :] = o.astype(o_ref.dtype)

    def chunk(c, slot, m, nxt, prev):
        if nxt == "next step":
            qq_next, next_parts = stacked(qn_ref[0, 0]), parts_next
        else:
            qq_next, next_parts = (stacked_queries(c + 1) if nxt else None), parts
        m_next = denom = acc = None
        for p in range(n_parts):
            if nxt:
                pm = scores_part(qq_next, 1 - slot, p, next_parts)
                m_next = pm if m_next is None else jnp.maximum(m_next, pm)
            if prev and p == 0:
                finish(c - 1, 1 - slot)
            e = jnp.exp2(st_scr[slot, p * kp:(p + 1) * kp, :] - m)
            ds = jnp.sum(e, axis=0, keepdims=True)
            _, vt_ref, off = parts[p]
            da = _dot(vt_ref[0, 0, :, off:off + kp], e.astype(BF16))
            denom = ds if denom is None else denom + ds
            acc = da if acc is None else acc + da
        acc_scr[slot] = acc
        den_scr[slot] = denom
        return m_next

    n_chunks = tq // qc
    assert n_chunks == 1 or n_chunks % 2 == 0
    if cross:
        @pl.when((pl.program_id(0) == 0) & (pl.program_id(1) == 0))
        def _():
            m_scr[...] = scores(0, 0)
        m = m_scr[...]
    else:
        m = scores(0, 0)
    if n_chunks > 1:
        m = chunk(0, 0, m, True, False)

        def pair(i, m):
            return chunk(2 * i + 2, 0, chunk(2 * i + 1, 1, m, True, True), True, True)
        m = lax.fori_loop(0, n_chunks // 2 - 1, pair, m)
        m = chunk(n_chunks - 1, 1, m, "next step" if cross else False, True)
        if cross:
            m_scr[...] = m
        finish(n_chunks - 1, 1)
    else:
        chunk(0, 0, m, False, False)
        finish(0, 0)


def _diff_attention(q, kv_srcs, lam_vec, subln, lam_init, tq=2048, qc=256):
    bsz, n_heads, lq, _ = q.shape
    tq = _row_tile(lq, tq)
    qc = _row_tile(tq, qc)
    lk_all = sum(k.shape[2] for k, _ in kv_srcs)
    ins = [lam_vec, subln.reshape(1, V_HEAD_DIM), q]
    specs = [pl.BlockSpec(lam_vec.shape, lambda b, h, i: (0, 0)),
             pl.BlockSpec((1, V_HEAD_DIM), lambda b, h, i: (0, 0)),
             pl.BlockSpec((1, 1, tq, V_HEAD_DIM), lambda b, h, i: (b, h, i, 0))]
    for k, vt in kv_srcs:
        lk = k.shape[2]
        ins += [k, vt]
        specs += [pl.BlockSpec((1, 1, lk, V_HEAD_DIM), lambda b, h, i: (b, h, 0, 0)),
                  pl.BlockSpec((1, 1, V_HEAD_DIM, lk), lambda b, h, i: (b, h, 0, 0))]
    cross = tq == lq and tq > qc
    if cross:
        last = bsz * n_heads - 1

        def nxt(b, h, i):
            f = jnp.minimum(b * n_heads + h + 1, last)
            return f // n_heads, f % n_heads, 0, 0
        ins += [q] + [k for k, _ in kv_srcs]
        specs += [pl.BlockSpec((1, 1, qc, V_HEAD_DIM), nxt)]
        specs += [pl.BlockSpec((1, 1, k.shape[2], V_HEAD_DIM), nxt) for k, _ in kv_srcs]
    return pl.pallas_call(
        functools.partial(_attn_kernel, n_kv=len(kv_srcs), lam_init=lam_init, tq=tq, cross=cross),
        grid=(bsz, n_heads, lq // tq),
        in_specs=specs,
        out_specs=pl.BlockSpec((1, 1, tq, V_HEAD_DIM), lambda b, h, i: (b, h, i, 0)),
        out_shape=jax.ShapeDtypeStruct((bsz, n_heads, lq, V_HEAD_DIM), BF16),
        scratch_shapes=[pltpu.VMEM((2, lk_all, 2 * qc), F32), pltpu.VMEM((2, V_HEAD_DIM, 2 * qc), F32),
                        pltpu.VMEM((2, 1, 2 * qc), F32), pltpu.VMEM((1, 2 * qc), F32)],
        compiler_params=_params("arbitrary", "arbitrary", "arbitrary"),
        name="diff_attention",
    )(*ins)


def _mlp_kernel(*refs, mix, mix_bias):
    refs = list(refs)
    if mix:
        y_ref, wo_ref = refs.pop(0), refs.pop(0)
        bo_ref = refs.pop(0) if mix_bias else None
        gmix_ref = refs.pop(0)
    h_ref, gpre_ref, gpost_ref, mod_ref, w1_ref, w2_ref, o_ref, hid_ref = refs
    x = h_ref[...]
    if mix:
        if len(y_ref.shape) == 4:
            yin = jnp.concatenate([y_ref[0, hd] for hd in range(y_ref.shape[1])], axis=1)
        else:
            yin = y_ref[...]
        y = _dot(yin, wo_ref[...])
        if mix_bias:
            y = y + bo_ref[...]
        x = x + mod_ref[0, 2:3, :] * _rms(y, gmix_ref[...])
    v = (_rms(x, gpre_ref[...]) * (1.0 + mod_ref[0, 4:5, :]) + mod_ref[0, 3:4, :]).astype(BF16)
    dff = w1_ref.shape[1]
    fc = 4 * MXU_DIM
    for c in range(dff // fc):
        hid = jnp.maximum(_dot(v, w1_ref[:, c * fc:(c + 1) * fc]), 0.0)
        hid_ref[:, c * fc:(c + 1) * fc] = (hid * hid).astype(BF16)
    y = _dot(hid_ref[...], w2_ref[...])
    o_ref[...] = x + mod_ref[0, 5:6, :] * _rms(y, gpost_ref[...])


def _mlp_sublayer(h, gpre, gpost, mod, w1, w2, *, rows_per_mod, mix=None, tm=1024):
    rows, d = h.shape
    dff = w1.shape[1]
    head_major = mix is not None and mix[0].ndim == 4
    tm = _row_tile(min(rows, rows_per_mod, mix[0].shape[2] if head_major else rows), tm)
    tile = pl.BlockSpec((tm, d), lambda i: (i, 0))
    vec = pl.BlockSpec((1, d), lambda i: (0, 0))
    ins, specs, mix_bias = [], [], False
    if mix is not None:
        y, wo, bo, gmix = mix
        mix_bias = bo is not None
        ins = [y, wo] + ([bo.reshape(1, d)] if mix_bias else []) + [gmix.reshape(1, d)]
        if head_major:
            per = y.shape[2] // tm
            y_spec = pl.BlockSpec((1, y.shape[1], tm, V_HEAD_DIM), lambda i: (i // per, 0, i % per, 0))
        else:
            y_spec = tile
        specs = [y_spec, _resident((d, d), lambda i: (0, 0))] + ([vec] if mix_bias else []) + [vec]
    return pl.pallas_call(
        functools.partial(_mlp_kernel, mix=mix is not None, mix_bias=mix_bias),
        grid=(rows // tm,),
        in_specs=specs + [tile, vec, vec,
                          pl.BlockSpec((1, N_MOD, d), lambda i: (i * tm // rows_per_mod, 0, 0)),
                          _resident((d, dff), lambda i: (0, 0)),
                          _resident((dff, d), lambda i: (0, 0))],
        out_specs=tile,
        out_shape=jax.ShapeDtypeStruct((rows, d), F32),
        scratch_shapes=[pltpu.VMEM((tm, dff), BF16)],
        compiler_params=_params("arbitrary"),
        name="mlp_sublayer",
    )(*ins, h, gpre.reshape(1, d), gpost.reshape(1, d), mod, w1, w2)


def _hyena_filter_kernel(z_ref, w1_ref, b1_ref, w2_ref, b2_ref, fr_ref, wo_ref, dl_ref, hf_ref, hb_ref):
    hi = lax.Precision.HIGHEST
    z = z_ref[...]
    freq = fr_ref[...]
    hdn = jnp.sin(freq * (jnp.dot(z, w1_ref[...], precision=hi, preferred_element_type=F32) + b1_ref[...]))
    for j in range(w2_ref.shape[0]):
        hdn = jnp.sin(freq * (jnp.dot(hdn, w2_ref[j], precision=hi, preferred_element_type=F32) + b2_ref[j]))
    h = jnp.dot(hdn, wo_ref[...], precision=hi, preferred_element_type=F32)
    d = hf_ref.shape[1]
    window = jnp.exp(-z[:, 0:1] * dl_ref[...])
    hf_ref[...] = h[:, :d] * window
    hb_ref[...] = h[:, d:] * window


def _hyena_filters(n, w1, b1, w2, b2, freq, w_out, d):
    bands = (HYENA_EMB_DIM - 1) // 2
    t = jnp.linspace(0.0, 1.0, n, dtype=F32)[:, None]
    w = 2.0 * math.pi * jnp.arange(n, dtype=F32)[:, None] / n
    f = jnp.linspace(1e-4, bands - 1, bands, dtype=F32)[None, :]
    z = jnp.concatenate([t, jnp.cos(f * w), -jnp.sin(f * w),
                         jnp.zeros((n, LANES - HYENA_EMB_DIM), F32)], axis=-1)
    w1p = jnp.concatenate([w1, jnp.zeros((LANES - HYENA_EMB_DIM, w1.shape[1]), F32)], axis=0)
    deltas = jnp.abs(jnp.linspace(math.log(HYENA_DECAY_TARGET) / HYENA_SLOW_DECAY_PCT,
                                  math.log(HYENA_DECAY_TARGET) / HYENA_FAST_DECAY_PCT, d, dtype=F32))
    order = w1.shape[1]
    tr = _row_tile(n, 256)
    full = lambda *shape: pl.BlockSpec(shape, lambda i: (0,) * len(shape))
    return pl.pallas_call(
        _hyena_filter_kernel,
        grid=(n // tr,),
        in_specs=[pl.BlockSpec((tr, LANES), lambda i: (i, 0)),
                  full(LANES, order), full(1, order), full(*w2.shape), full(w2.shape[0], 1, order),
                  full(1, order), full(order, 2 * d), full(1, d)],
        out_specs=[pl.BlockSpec((tr, d), lambda i: (i, 0))] * 2,
        out_shape=[jax.ShapeDtypeStruct((n, d), F32)] * 2,
        compiler_params=_params("arbitrary"),
        name="hyena_filters",
    )(z, w1p, b1.reshape(1, order), w2, b2.reshape(w2.shape[0], 1, order), freq.reshape(1, order),
      w_out, deltas.reshape(1, d))


def _dft_matrices(n):
    k = jnp.arange(n, dtype=jnp.int32)
    blk = min(n, LANES)

    def tables(t):
        ang = ((k[:, None] * t[None, :]) % (2 * n)).astype(F32) * (math.pi / n)
        return jnp.cos(ang), jnp.sin(ang)

    ca, sa = tables(jnp.arange(n // blk, dtype=jnp.int32) * blk)
    cb, sb = tables(jnp.arange(blk, dtype=jnp.int32))
    cmat = ca[:, :, None] * cb[:, None, :] - sa[:, :, None] * sb[:, None, :]
    smat = sa[:, :, None] * cb[:, None, :] + ca[:, :, None] * sb[:, None, :]
    return cmat.reshape(n, n).astype(BF16), smat.reshape(n, n).astype(BF16)


def _alt_sign(n):
    row = lax.broadcasted_iota(jnp.int32, (n, 1), 0)
    return jnp.where(row % 2 == 0, 1.0, -1.0).astype(F32), row


def _hyena_spec_kernel(c_ref, s_ref, hf_ref, hb_ref, kr_ref, ks_ref, kn_ref):
    m = c_ref.shape[0]
    n = hf_ref.shape[0]
    nb = n // m
    sgn, row = _alt_sign(m)
    wk = jnp.where(row == 0, 0.5 / m, 1.0 / m)

    def transforms(seg):
        sb = seg.astype(BF16)
        first = sb[0:1, :].astype(F32)
        ct, st = _dot(c_ref[...], sb), _dot(s_ref[...], sb)
        at = jnp.sum(sgn * seg, axis=0, keepdims=True)
        return ct, st, at, ct - first, at - seg[0:1, :]

    fwd = [transforms(hf_ref[j * m:(j + 1) * m, :]) for j in range(nb)]
    bwd = [transforms(hb_ref[j * m:(j + 1) * m, :]) for j in range(nb)]
    for lag in range(-(nb - 1), nb):
        if lag == 0:
            kr, ks, kn = fwd[0][0] + bwd[0][3], fwd[0][1] - bwd[0][1], fwd[0][2] + bwd[0][4]
        else:
            cur, prv = (fwd[lag], fwd[lag - 1]) if lag > 0 else (bwd[-lag], bwd[-lag - 1])
            kr = cur[0] + sgn * prv[3]
            ks = cur[1] + sgn * prv[1]
            kn = cur[2] + prv[4]
            if lag < 0:
                ks = -ks
        kr_ref[lag + nb - 1] = kr * wk
        ks_ref[lag + nb - 1] = ks * wk
        kn_ref[lag + nb - 1] = kn * (0.5 / m)


def _hyena_spectrum(cmat, smat, hf, hb):
    n, d = hf.shape
    m = cmat.shape[0]
    assert n % m == 0 and m % 2 == 0
    nl = 2 * (n // m) - 1
    cb = min(d, MXU_DIM)
    return pl.pallas_call(
        _hyena_spec_kernel,
        grid=(d // cb,),
        in_specs=[_resident((m, m), lambda j: (0, 0)), _resident((m, m), lambda j: (0, 0)),
                  pl.BlockSpec((n, cb), lambda j: (0, j)), pl.BlockSpec((n, cb), lambda j: (0, j))],
        out_specs=[pl.BlockSpec((nl, m, cb), lambda j: (0, 0, j)), pl.BlockSpec((nl, m, cb), lambda j: (0, 0, j)),
                   pl.BlockSpec((nl, 1, cb), lambda j: (0, 0, j))],
        out_shape=[jax.ShapeDtypeStruct((nl, m, d), F32), jax.ShapeDtypeStruct((nl, m, d), F32),
                   jax.ShapeDtypeStruct((nl, 1, d), F32)],
        compiler_params=_params("arbitrary"),
        name="hyena_spectrum",
    )(cmat, smat, hf, hb)


def _hyena_conv_kernel(x0_ref, vv_ref, c_ref, s_ref, kr_ref, ks_ref, kn_ref, db_ref, o_ref,
                       vb_scr, fr_scr, fs_scr, p_scr, q_scr):
    n, cb = o_ref.shape[1], o_ref.shape[2]
    m = c_ref.shape[0]
    nb = n // m
    sgn, _ = _alt_sign(m)

    def transform(s):
        fn = []
        for jb in range(nb):
            blk = vv_ref[s, jb * m:(jb + 1) * m, :]
            vb_scr[s, :, jb * cb:(jb + 1) * cb] = blk
            fn.append(jnp.sum(blk.astype(F32) * sgn, axis=0, keepdims=True))
        fr_scr[s] = _dot(c_ref[...], vb_scr[s])
        fs_scr[s] = _dot(s_ref[...], vb_scr[s])
        return fn

    rc = min(m, 32)

    def spectral(s, fn):
        nyq = []
        for ib in range(nb):
            nyq.append(functools.reduce(jnp.add, [fn[jb] * kn_ref[ib - jb + nb - 1] for jb in range(nb)]))
            for r0 in range(0, m, rc):
                rows = slice(r0, r0 + rc)
                p = q = None
                for jb in range(nb):
                    cols = slice(jb * cb, (jb + 1) * cb)
                    fr, fs = fr_scr[s, rows, cols], fs_scr[s, rows, cols]
                    kr, ks = kr_ref[ib - jb + nb - 1, rows, :], ks_ref[ib - jb + nb - 1, rows, :]
                    dp, dq = fr * kr - fs * ks, fr * ks + fs * kr
                    p, q = (dp, dq) if p is None else (p + dp, q + dq)
                p_scr[s, rows, ib * cb:(ib + 1) * cb] = p.astype(BF16)
                q_scr[s, rows, ib * cb:(ib + 1) * cb] = q.astype(BF16)
        return nyq

    def inverse(s):
        return _dot(c_ref[...], p_scr[s]) + _dot(s_ref[...], q_scr[s])

    def emit(s, y, nyq):
        for ib in range(nb):
            rows = slice(ib * m, (ib + 1) * m)
            yb = y[:, ib * cb:(ib + 1) * cb] + sgn * nyq[ib] + vv_ref[s, rows, :].astype(F32) * db_ref[...]
            o_ref[s, rows, :] = (yb * x0_ref[s, rows, :].astype(F32)).astype(o_ref.dtype)

    seqs = range(o_ref.shape[0])
    fn = [transform(s) for s in seqs]
    y, nyq = [], []
    for s in seqs:
        nyq.append(spectral(s, fn[s]))
        y.append(inverse(s))
    for s in seqs:
        emit(s, y[s], nyq[s])


def _hyena_conv(x0, vv, cmat, smat, kr, ks, kn, d_bias):
    bsz, n, d = vv.shape
    cb = min(d, MXU_DIM)
    nb = d // cb
    m, nl = cmat.shape[0], kr.shape[0]
    ns = 1
    wide = (ns, m, (n // m) * cb)
    seq = pl.BlockSpec((ns, n, cb), lambda j, b: (b, 0, j))
    spec = pl.BlockSpec((nl, m, cb), lambda j, b: (0, 0, j))
    nyq = pl.BlockSpec((nl, 1, cb), lambda j, b: (0, 0, j))
    vec = pl.BlockSpec((1, cb), lambda j, b: (0, j))
    return pl.pallas_call(
        _hyena_conv_kernel,
        grid=(nb, bsz // ns),
        in_specs=[seq, seq, _resident((m, m), lambda j, b: (0, 0)), _resident((m, m), lambda j, b: (0, 0)),
                  spec, spec, nyq, vec],
        out_specs=seq,
        out_shape=jax.ShapeDtypeStruct((bsz, n, d), BF16),
        scratch_shapes=[pltpu.VMEM(wide, BF16), pltpu.VMEM(wide, F32), pltpu.VMEM(wide, F32),
                        pltpu.VMEM(wide, BF16), pltpu.VMEM(wide, BF16)],
        compiler_params=_params("arbitrary", "arbitrary"),
        name="hyena_conv",
    )(x0, vv, cmat, smat, kr, ks, kn, d_bias.reshape(1, d))


def _window_dft(nw):
    half = nw // 2
    t = jnp.arange(nw, dtype=jnp.int32)
    ang = ((jnp.arange(half, dtype=jnp.int32)[:, None] * t[None, :]) % nw).astype(F32) * (2.0 * math.pi / nw)
    nyq = jnp.where(t % 2 == 0, 1.0, -1.0).astype(F32)[None, :]
    return jnp.concatenate([jnp.cos(ang), nyq, jnp.sin(ang)[1:]], axis=0)


def _tap_spectrum_kernel(tr_ref, ts_ref, td_ref, w_ref, g_ref):
    hi = lax.Precision.HIGHEST
    for i, t_ref in enumerate((tr_ref, ts_ref, td_ref)):
        g_ref[i] = jnp.dot(t_ref[...], w_ref[...], precision=hi, preferred_element_type=F32)


def _tap_spectrum(w_dw, nw):
    width, d = w_dw.shape
    reach, half = width // 2, nw // 2
    kpad = -width % SUBLANES
    lag = reach - jnp.arange(width + kpad, dtype=jnp.int32)
    ang = ((jnp.arange(half, dtype=jnp.int32)[:, None] * lag[None, :]) % nw).astype(F32) * (2.0 * math.pi / nw)
    wf = jnp.where(jnp.arange(half) == 0, 1.0 / nw, 2.0 / nw).astype(F32)[:, None]
    tr = jnp.cos(ang) * wf
    ts = jnp.sin(ang) * wf
    nyq = jnp.where(lag % 2 == 0, 1.0 / nw, -1.0 / nw).astype(F32)[None, :]
    td = jnp.concatenate([nyq, tr[1:]], axis=0)
    wp = jnp.concatenate([w_dw, jnp.zeros((kpad, d), F32)], axis=0)
    full = lambda a: pl.BlockSpec(a.shape, lambda: (0,) * a.ndim)
    return pl.pallas_call(
        _tap_spectrum_kernel,
        in_specs=[full(tr), full(ts), full(td), full(wp)],
        out_specs=pl.BlockSpec((3, half, d), lambda: (0, 0, 0)),
        out_shape=jax.ShapeDtypeStruct((3, half, d), F32),
        name="tap_spectrum",
    )(tr, ts, td, wp)


def _conformer_post_kernel(x_ref, a_ref, ai_ref, gt_ref, bd_ref, lg_ref, lb_ref, w_ref, b_ref, h_ref, g_ref,
                           mod_ref, o_ref, xp_scr, ys_scr):
    n, d = x_ref.shape[1], x_ref.shape[2]
    nw = a_ref.shape[0]
    half = nw // 2
    tm = o_ref.shape[0]
    j = pl.program_id(1)

    @pl.when(j == 0)
    def _():
        xp_scr[0:CONV_HALO, :] = jnp.zeros((CONV_HALO, d), BF16)
        xp_scr[CONV_HALO:CONV_HALO + n, :] = x_ref[0]
        xp_scr[CONV_HALO + n:, :] = jnp.zeros((CONV_HALO, d), BF16)

    tw = nw - 2 * CONV_HALO
    base = pl.multiple_of(j * tm, tm)

    def forward(w):
        return _dot(a_ref[...], xp_scr[pl.ds(base + w * tw, nw), :])

    def spectral(w, f):
        xr, xs = f[:half], f[half:]
        ys_scr[w, 0:half, :] = (xr * gt_ref[0] - xs * gt_ref[1]).astype(BF16)
        ys_scr[w, half:, :] = (xr * gt_ref[1] + xs * gt_ref[2]).astype(BF16)

    def inverse(w):
        return _dot(ai_ref[...], ys_scr[w]) + bd_ref[...]

    def activate(z):
        mu = jnp.mean(z, axis=-1, keepdims=True)
        zc = z - mu
        var = jnp.mean(zc * zc, axis=-1, keepdims=True)
        zn = zc * lax.rsqrt(var + LN_EPS) * lg_ref[...] + lb_ref[...]
        return (zn / (1.0 + jnp.exp(-zn))).astype(BF16)

    def project(act):
        return _dot(act, w_ref[...]) + b_ref[...]

    def store(w, y):
        rows = slice(w * tw, (w + 1) * tw)
        o_ref[rows, :] = h_ref[rows, :] + mod_ref[0, 2:3, :] * _rms(y, g_ref[...])

    wins = range(tm // tw)
    f = [forward(w) for w in wins]
    z = []
    for w in wins:
        spectral(w, f[w])
        z.append(inverse(w))
    y = []
    for w in wins:
        y.append(project(activate(z[w])))
    for w in wins:
        store(w, y[w])


def _conformer_post(x, w_dw, b_dw, ln_g, ln_b, w, b, h, g, mod, *, per_batch_mod, tm=1024, tw=MXU_DIM):
    bsz, n, d = x.shape
    assert w_dw.shape[0] // 2 < CONV_HALO
    tm = _row_tile(n, tm)
    tw = _row_tile(tm, tw)
    per = n // tm
    nw = tw + 2 * CONV_HALO
    amat = _window_dft(nw)
    gtab = _tap_spectrum(w_dw, nw)
    vec = pl.BlockSpec((1, d), lambda bi, j: (0, 0))
    return pl.pallas_call(
        _conformer_post_kernel,
        grid=(bsz, per),
        in_specs=[pl.BlockSpec((1, n, d), lambda bi, j: (bi, 0, 0)),
                  _resident((nw, nw), lambda bi, j: (0, 0)), _resident((tw, nw), lambda bi, j: (0, 0)),
                  _resident((3, nw // 2, d), lambda bi, j: (0, 0, 0)), vec, vec, vec,
                  _resident((d, d), lambda bi, j: (0, 0)), vec,
                  pl.BlockSpec((tm, d), lambda bi, j: (bi * per + j, 0)), vec,
                  pl.BlockSpec((1, N_MOD, d), lambda bi, j: (bi if per_batch_mod else 0, 0, 0))],
        out_specs=pl.BlockSpec((tm, d), lambda bi, j: (bi * per + j, 0)),
        out_shape=jax.ShapeDtypeStruct((bsz * n, d), F32),
        scratch_shapes=[pltpu.VMEM((n + 2 * CONV_HALO, d), BF16), pltpu.VMEM((tm // tw, nw, d), BF16)],
        compiler_params=_params("arbitrary", "arbitrary"),
        name="conformer_post",
    )(x, amat.astype(BF16), amat.T[CONV_HALO:CONV_HALO + tw].astype(BF16), gtab, b_dw.reshape(1, d),
      ln_g.reshape(1, d), ln_b.reshape(1, d), w, b.reshape(1, d), h, g.reshape(1, d), mod)


def kernel(x, c, ctx, c_ctx, w_mod, b_mod, norm_mix_pre, norm_mix_post, norm_mlp_pre, norm_mlp_post, w_mlp_in, w_mlp_out, attn_w_qkv, attn_w_out, attn_lambda, attn_subln, hy_w_in, hy_b_in, hy_w_short, hy_b_short, hy_filt_w1, hy_filt_b1, hy_filt_w2, hy_filt_b2, hy_filt_freq, hy_filt_w_out, hy_bias, hy_w_out, hy_b_out, cv_w_pw1, cv_b_pw1, cv_w_dw, cv_b_dw, cv_ln_g, cv_ln_b, cv_w_pw2, cv_b_pw2):
    bsz, seq, d = x.shape
    n_ctx = ctx.shape[1]
    depth = w_mod.shape[0]
    assert bsz + 1 <= MOD_ROWS and d % (2 * MXU_DIM) == 0

    cc = jnp.concatenate([c, c_ctx[None, :], jnp.zeros((MOD_ROWS - bsz - 1, d), F32)], axis=0)
    mods = _mod_vectors(cc, w_mod, b_mod)
    rope_tabs = _rope_tables(seq)

    h_lat = x.reshape(bsz * seq, d)
    h_ctx = ctx.reshape(bsz * n_ctx, d)
    lat = dict(rows_per_mod=seq)
    cx = dict(rows_per_mod=bsz * n_ctx)
    for i in range(depth):
        last = i == depth - 1
        kind, j = i % N_MIXERS, i // N_MIXERS
        ctx_out = not last
        mod_l = mods[i, :bsz].reshape(bsz, N_MOD, d)
        mod_c = mods[i, bsz:bsz + 1].reshape(1, N_MOD, d)
        g_pre, g_post = norm_mix_pre[i], norm_mix_post[i]
        mix_l = mix_c = None
        if kind == 0:
            lam_init = 0.8 - 0.6 * math.exp(-0.3 * i)
            w_qkv, w_o = attn_w_qkv[j].astype(BF16), attn_w_out[j].astype(BF16)
            q_l, k_l, vt_l = _pre_project(h_lat, g_pre, mod_l, w_qkv, None, kind="attn", seq=seq,
                                          rope_tabs=rope_tabs, **lat)
            q_c, k_c, vt_c = _pre_project(h_ctx, g_pre, mod_c, w_qkv, None, kind="attn", seq=n_ctx, **cx)
            o_l = _diff_attention(q_l, [(k_c, vt_c), (k_l, vt_l)], attn_lambda[j], attn_subln[j], lam_init)
            mix_l = (o_l, w_o, None, g_post)
            if ctx_out:
                o_c = _diff_attention(q_c, [(k_c, vt_c)], attn_lambda[j], attn_subln[j], lam_init)
                mix_c = (o_c, w_o, None, g_post)
        elif kind == 1:
            w_in, w_o = hy_w_in[j].astype(BF16), hy_w_out[j].astype(BF16)

            def hyena(h, mod, n, rpm):
                x0, vv = _pre_project(h, g_pre, mod, w_in, hy_b_in[j], kind="hyena", rows_per_mod=rpm, seq=n,
                                      short=(hy_w_short[j], hy_b_short[j]))
                hf, hb = _hyena_filters(n, hy_filt_w1[j], hy_filt_b1[j], hy_filt_w2[j], hy_filt_b2[j],
                                        hy_filt_freq[j], hy_filt_w_out[j], d)
                cmat, smat = _dft_matrices(min(n, HYENA_BLOCK))
                kr, ks, kn = _hyena_spectrum(cmat, smat, hf, hb)
                g = _hyena_conv(x0.reshape(bsz, n, d), vv.reshape(bsz, n, d), cmat, smat, kr, ks, kn, hy_bias[j])
                return (g.reshape(bsz * n, d), w_o, hy_b_out[j], g_post)

            mix_l = hyena(h_lat, mod_l, seq, seq)
            if ctx_out:
                mix_c = hyena(h_ctx, mod_c, n_ctx, bsz * n_ctx)
        else:
            w_1, w_2 = cv_w_pw1[j].astype(BF16), cv_w_pw2[j].astype(BF16)

            def conformer(h, mod, n, rpm, per_batch_mod):
                a = _pre_project(h, g_pre, mod, w_1, cv_b_pw1[j], kind="conformer", rows_per_mod=rpm)
                return _conformer_post(a.reshape(bsz, n, d), cv_w_dw[j], cv_b_dw[j], cv_ln_g[j], cv_ln_b[j],
                                       w_2, cv_b_pw2[j], h, g_post, mod, per_batch_mod=per_batch_mod)

            h_lat = conformer(h_lat, mod_l, seq, seq, True)
            if ctx_out:
                h_ctx = conformer(h_ctx, mod_c, n_ctx, bsz * n_ctx, False)
        w1, w2 = w_mlp_in[i].astype(BF16), w_mlp_out[i].astype(BF16)
        h_lat = _mlp_sublayer(h_lat, norm_mlp_pre[i], norm_mlp_post[i], mod_l, w1, w2, mix=mix_l, **lat)
        if ctx_out:
            h_ctx = _mlp_sublayer(h_ctx, norm_mlp_pre[i], norm_mlp_post[i], mod_c, w1, w2, mix=mix_c, **cx)
    return h_lat.reshape(bsz, seq, d)
```

```python
import functools
import math

import jax
import jax.numpy as jnp
from jax import lax
from jax.experimental import pallas as pl
from jax.experimental.pallas import tpu as pltpu

F32 = jnp.float32
BF16 = jnp.bfloat16

N_MIXERS = 3
N_MOD = 6
HEAD_DIM = 64
V_HEAD_DIM = 2 * HEAD_DIM
GRID_W = 64
ROPE_THETA = 10000.0
NORM_EPS = 1e-6
LN_EPS = 1e-5
HYENA_EMB_DIM = 33
HYENA_DECAY_TARGET = 1e-2
HYENA_FAST_DECAY_PCT = 0.3
HYENA_SLOW_DECAY_PCT = 1.5

LANES = 128
SUBLANES = 8
MXU_DIM = 256
VMEM_LIMIT_BYTES = 56 * 1024 * 1024
MOD_ROWS = 24
CONV_HALO = 16
HYENA_BLOCK = 512


def _params(*sem):
    return pltpu.CompilerParams(dimension_semantics=sem, vmem_limit_bytes=VMEM_LIMIT_BYTES)


def _resident(shape, index_map):
    return pl.BlockSpec(shape, index_map, pipeline_mode=pl.Buffered(1))


def _row_tile(rows, want):
    t = min(rows, want)
    assert rows % t == 0
    return t


def _rms(x, g):
    return x * lax.rsqrt(jnp.mean(x * x, axis=-1, keepdims=True) + NORM_EPS) * g


def _dot(a, b):
    return jnp.dot(a, b, preferred_element_type=F32)


def _mod_kernel(c_ref, w_ref, b_ref, o_ref):
    c = c_ref[...]
    s = (c / (1.0 + jnp.exp(-c))).astype(BF16)
    o_ref[0] = _dot(s, w_ref[0].astype(BF16)) + b_ref[0]


def _mod_vectors(cc, w_mod, b_mod):
    depth, d, n = w_mod.shape
    tn = 1024
    return pl.pallas_call(
        _mod_kernel,
        grid=(depth, n // tn),
        in_specs=[pl.BlockSpec((MOD_ROWS, d), lambda i, j: (0, 0)),
                  pl.BlockSpec((1, d, tn), lambda i, j: (i, 0, j)),
                  pl.BlockSpec((1, 1, tn), lambda i, j: (i, 0, j))],
        out_specs=pl.BlockSpec((1, MOD_ROWS, tn), lambda i, j: (i, 0, j)),
        out_shape=jax.ShapeDtypeStruct((depth, MOD_ROWS, n), F32),
        compiler_params=_params("arbitrary", "arbitrary"),
        name="mod_vectors",
    )(cc, w_mod, b_mod.reshape(depth, 1, n))


def _rope_group(z, cos, sin, first_half):
    partner = jnp.where(first_half, pltpu.roll(z, LANES - 16, axis=1), pltpu.roll(z, 16, axis=1))
    return z * cos + partner * sin


def _pre_kernel(*refs, kind, d, rope, per):
    h_ref, g_ref, mod_ref, w_ref = refs[:4]
    rest = list(refs[4:])
    b_ref = rest.pop(0) if kind != "attn" else None
    cos_ref, sin_ref = (rest.pop(0), rest.pop(0)) if rope else (None, None)
    if kind == "attn":
        oq_ref, ok_ref, ovt_ref = rest
    elif kind == "hyena":
        hp_ref, hn_ref, ws_ref, bs_ref, ox0_ref, ovv_ref = rest
    else:
        (o_ref,) = rest

    def modulated(x):
        return _rms(x, g_ref[...]) * (1.0 + mod_ref[0, 1:2, :]) + mod_ref[0, 0:1, :]

    u = modulated(h_ref[...])
    ub = u.astype(BF16)
    nc = 2 * MXU_DIM
    if kind == "hyena":
        tm = u.shape[0]
        i = pl.program_id(0)
        rowx = lax.broadcasted_iota(jnp.int32, (tm + 2 * SUBLANES, 1), 0)
        outside = ((rowx < SUBLANES) & (i % per == 0)) | ((rowx >= tm + SUBLANES) & (i % per == per - 1))
        keep = jnp.where(outside, 0.0, 1.0)
        u_ext = jnp.concatenate([modulated(hp_ref[...]), u, modulated(hn_ref[...])], axis=0).astype(BF16)

        def conv_cols(c0):
            cols = slice(c0, c0 + nc)
            z = (_dot(u_ext, w_ref[:, cols]) + b_ref[:, cols]) * keep
            prev = pltpu.roll(z, 1, axis=0)[SUBLANES:SUBLANES + tm]
            nxt = pltpu.roll(z, tm + 2 * SUBLANES - 1, axis=0)[SUBLANES:SUBLANES + tm]
            return (ws_ref[0:1, cols] * prev + ws_ref[1:2, cols] * z[SUBLANES:SUBLANES + tm]
                    + ws_ref[2:3, cols] * nxt + bs_ref[:, cols])

        for c in range(d // nc):
            ox0_ref[:, c * nc:(c + 1) * nc] = conv_cols(c * nc).astype(ox0_ref.dtype)
            ovv_ref[:, c * nc:(c + 1) * nc] = (conv_cols(2 * d + c * nc) * conv_cols(d + c * nc)).astype(ovv_ref.dtype)
        return
    if kind == "conformer":
        for c in range(d // nc):
            a = _dot(ub, w_ref[:, c * nc:(c + 1) * nc]) + b_ref[:, c * nc:(c + 1) * nc]
            g = _dot(ub, w_ref[:, d + c * nc:d + (c + 1) * nc]) + b_ref[:, d + c * nc:d + (c + 1) * nc]
            o_ref[:, c * nc:(c + 1) * nc] = (a / (1.0 + jnp.exp(-g))).astype(o_ref.dtype)
        return
    if rope:
        lane = lax.broadcasted_iota(jnp.int32, (1, LANES), 1)
        first_half = (lane % 32) < 16
        cos, sin = cos_ref[...], sin_ref[...]
    for c in range(3 * d // nc):
        z = _dot(ub, w_ref[:, c * nc:(c + 1) * nc])
        section = c * nc // d
        for gi in range(nc // LANES):
            head = (c * nc - section * d) // LANES + gi
            zg = z[:, gi * LANES:(gi + 1) * LANES]
            if rope and section < 2:
                zg = _rope_group(zg, cos, sin, first_half)
            if section == 0:
                oq_ref[0, head] = (zg * (HEAD_DIM ** -0.5 * math.log2(math.e))).astype(oq_ref.dtype)
            elif section == 1:
                ok_ref[0, head] = zg.astype(ok_ref.dtype)
            else:
                ovt_ref[0, head] = zg.T.astype(ovt_ref.dtype)


def _pre_project(h, g, mod, w, b, *, kind, rows_per_mod, seq=None, rope_tabs=None, short=None, tm=1024):
    rows, d = h.shape
    n_in = w.shape[1]
    n_out = d if kind == "conformer" else n_in
    tm = _row_tile(min(rows, rows_per_mod, seq or rows), tm)
    rope = rope_tabs is not None
    per = seq // tm if seq else None
    if kind == "attn":
        nh = d // V_HEAD_DIM
        qk_spec = pl.BlockSpec((1, nh, tm, V_HEAD_DIM), lambda i: (i // per, 0, i % per, 0))
        out_specs = [qk_spec, qk_spec, pl.BlockSpec((1, nh, V_HEAD_DIM, tm), lambda i: (i // per, 0, 0, i % per))]
        qk_shape = jax.ShapeDtypeStruct((rows // seq, nh, seq, V_HEAD_DIM), BF16)
        out_shape = [qk_shape, qk_shape, jax.ShapeDtypeStruct((rows // seq, nh, V_HEAD_DIM, seq), BF16)]
    elif kind == "hyena":
        out_specs = [pl.BlockSpec((tm, d), lambda i: (i, 0))] * 2
        out_shape = [jax.ShapeDtypeStruct((rows, d), BF16)] * 2
    else:
        out_specs = pl.BlockSpec((tm, n_out), lambda i: (i, 0))
        out_shape = jax.ShapeDtypeStruct((rows, n_out), BF16)
    ins = [h, g.reshape(1, d), mod, w]
    specs = [pl.BlockSpec((tm, d), lambda i: (i, 0)),
             pl.BlockSpec((1, d), lambda i: (0, 0)),
             pl.BlockSpec((1, N_MOD, d), lambda i: (i * tm // rows_per_mod, 0, 0)),
             _resident((d, n_in), lambda i: (0, 0))]
    if kind != "attn":
        ins.append(b.reshape(1, n_in))
        specs.append(pl.BlockSpec((1, n_in), lambda i: (0, 0)))
    if rope:
        assert rope_tabs[0].shape[0] == seq
        ins += list(rope_tabs)
        specs += [pl.BlockSpec((tm, LANES), lambda i: (i % per, 0))] * 2
    if kind == "hyena":
        w_short, b_short = short
        assert w_short.shape[0] == 3 and tm % SUBLANES == 0
        groups, last = tm // SUBLANES, rows // SUBLANES - 1
        ins += [h, h, w_short, b_short.reshape(1, n_in)]
        specs += [pl.BlockSpec((SUBLANES, d), lambda i: (jnp.maximum(i * groups - 1, 0), 0)),
                  pl.BlockSpec((SUBLANES, d), lambda i: (jnp.minimum((i + 1) * groups, last), 0)),
                  pl.BlockSpec((3, n_in), lambda i: (0, 0)), pl.BlockSpec((1, n_in), lambda i: (0, 0))]
    return pl.pallas_call(
        functools.partial(_pre_kernel, kind=kind, d=d, rope=rope, per=per),
        grid=(rows // tm,),
        in_specs=specs,
        out_specs=out_specs,
        out_shape=out_shape,
        compiler_params=_params("arbitrary"),
        name=f"pre_{kind}",
    )(*ins)


def _rope_tables(seq):
    t = jnp.arange(seq)
    pos = jnp.stack([(t // GRID_W).astype(F32), (t % GRID_W).astype(F32)], axis=1)
    half = HEAD_DIM // 4
    inv_freq = ROPE_THETA ** (-jnp.arange(half, dtype=F32) / half)
    ang = pos[:, :, None] * inv_freq[None, None, :]
    cos = jnp.concatenate([jnp.cos(ang), jnp.cos(ang)], axis=-1).reshape(seq, HEAD_DIM)
    sin = jnp.concatenate([-jnp.sin(ang), jnp.sin(ang)], axis=-1).reshape(seq, HEAD_DIM)
    rep = LANES // HEAD_DIM
    return jnp.tile(cos, (1, rep)), jnp.tile(sin, (1, rep))


def _attn_kernel(*refs, n_kv, lam_init, tq, cross):
    lam_ref, sub_ref, q_ref = refs[:3]
    k_refs = refs[3:3 + 2 * n_kv:2]
    vt_refs = refs[4:4 + 2 * n_kv:2]
    rest = list(refs[3 + 2 * n_kv:])
    qn_ref, kn_refs = (rest.pop(0), [rest.pop(0) for _ in range(n_kv)]) if cross else (None, None)
    o_ref, st_scr, acc_scr, den_scr, m_scr = rest
    lk_all = st_scr.shape[1]
    qc = st_scr.shape[2] // 2
    kp = min(MXU_DIM, *(k_ref.shape[2] for k_ref in k_refs))
    parts = [(k_ref, vt_ref, o) for k_ref, vt_ref in zip(k_refs, vt_refs) for o in range(0, k_ref.shape[2], kp)]
    parts_next = [(kn_ref, None, o) for kn_ref in kn_refs for o in range(0, kn_ref.shape[2], kp)] if cross else None
    n_parts = len(parts)
    assert n_parts * kp == lk_all

    lv = lam_ref[...]
    lam = (jnp.exp(jnp.sum(lv[0:1] * lv[1:2], axis=1, keepdims=True))
           - jnp.exp(jnp.sum(lv[2:3] * lv[3:4], axis=1, keepdims=True)) + lam_init)
    lane = lax.broadcasted_iota(jnp.int32, (1, V_HEAD_DIM), 1)

    def stacked(q):
        zero = jnp.zeros_like(q)
        return jnp.concatenate([jnp.where(lane < HEAD_DIM, q, zero), jnp.where(lane >= HEAD_DIM, q, zero)], axis=0)

    def stacked_queries(c):
        return stacked(q_ref[0, 0, pl.ds(pl.multiple_of(c * qc, qc), qc), :])

    def scores_part(qq, slot, p, key_parts):
        k_ref, _, off = key_parts[p]
        st = lax.dot_general(k_ref[0, 0, off:off + kp, :], qq, (((1,), (1,)), ((), ())),
                             preferred_element_type=F32)
        st_scr[slot, p * kp:(p + 1) * kp, :] = st
        return jnp.max(st, axis=0, keepdims=True)

    def scores(c, slot):
        qq = stacked_queries(c)
        return functools.reduce(jnp.maximum, [scores_part(qq, slot, p, parts) for p in range(n_parts)])

    def finish(c, slot):
        acc = acc_scr[slot]
        r = 1.0 / den_scr[slot]
        ot = acc[:, :qc] * r[:, :qc] - acc[:, qc:] * (lam * r[:, qc:])
        o = _rms(ot.T, sub_ref[...]) * (1.0 - lam_init)
        o_ref[0, 0, pl.ds(pl.multiple_of(c * qc, qc), qc), :] = o.astype(o_ref.dtype)

    def chunk(c, slot, m, nxt, prev):
        if nxt == "next step":
            qq_next, next_parts = stacked(qn_ref[0, 0]), parts_next
        else:
            qq_next, next_parts = (stacked_queries(c + 1) if nxt else None), parts
        m_next = denom = acc = None
        for p in range(n_parts):
            if nxt:
                pm = scores_part(qq_next, 1 - slot, p, next_parts)
                m_next = pm if m_next is None else jnp.maximum(m_next, pm)
            if prev and p == 0:
                finish(c - 1, 1 - slot)
            e = jnp.exp2(st_scr[slot, p * kp:(p + 1) * kp, :] - m)
            ds = jnp.sum(e, axis=0, keepdims=True)
            _, vt_ref, off = parts[p]
            da = _dot(vt_ref[0, 0, :, off:off + kp], e.astype(BF16))
            denom = ds if denom is None else denom + ds
            acc = da if acc is None else acc + da
        acc_scr[slot] = acc
        den_scr[slot] = denom
        return m_next

    n_chunks = tq // qc
    assert n_chunks == 1 or n_chunks % 2 == 0
    if cross:
        @pl.when((pl.program_id(0) == 0) & (pl.program_id(1) == 0))
        def _():
            m_scr[...] = scores(0, 0)
        m = m_scr[...]
    else:
        m = scores(0, 0)
    if n_chunks > 1:
        m = chunk(0, 0, m, True, False)

        def pair(i, m):
            return chunk(2 * i + 2, 0, chunk(2 * i + 1, 1, m, True, True), True, True)
        m = lax.fori_loop(0, n_chunks // 2 - 1, pair, m)
        m = chunk(n_chunks - 1, 1, m, "next step" if cross else False, True)
        if cross:
            m_scr[...] = m
        finish(n_chunks - 1, 1)
    else:
        chunk(0, 0, m, False, False)
        finish(0, 0)


def _diff_attention(q, kv_srcs, lam_vec, subln, lam_init, tq=2048, qc=256):
    bsz, n_heads, lq, _ = q.shape
    tq = _row_tile(lq, tq)
    qc = _row_tile(tq, qc)
    lk_all = sum(k.shape[2] for k, _ in kv_srcs)
    ins = [lam_vec, subln.reshape(1, V_HEAD_DIM), q]
    specs = [pl.BlockSpec(lam_vec.shape, lambda b, h, i: (0, 0)),
             pl.BlockSpec((1, V_HEAD_DIM), lambda b, h, i: (0, 0)),
             pl.BlockSpec((1, 1, tq, V_HEAD_DIM), lambda b, h, i: (b, h, i, 0))]
    for k, vt in kv_srcs:
        lk = k.shape[2]
        ins += [k, vt]
        specs += [pl.BlockSpec((1, 1, lk, V_HEAD_DIM), lambda b, h, i: (b, h, 0, 0)),
                  pl.BlockSpec((1, 1, V_HEAD_DIM, lk), lambda b, h, i: (b, h, 0, 0))]
    cross = tq == lq and tq > qc
    if cross:
        last = bsz * n_heads - 1

        def nxt(b, h, i):
            f = jnp.minimum(b * n_heads + h + 1, last)
            return f // n_heads, f % n_heads, 0, 0
        ins += [q] + [k for k, _ in kv_srcs]
        specs += [pl.BlockSpec((1, 1, qc, V_HEAD_DIM), nxt)]
        specs += [pl.BlockSpec((1, 1, k.shape[2], V_HEAD_DIM), nxt) for k, _ in kv_srcs]
    return pl.pallas_call(
        functools.partial(_attn_kernel, n_kv=len(kv_srcs), lam_init=lam_init, tq=tq, cross=cross),
        grid=(bsz, n_heads, lq // tq),
        in_specs=specs,
        out_specs=pl.BlockSpec((1, 1, tq, V_HEAD_DIM), lambda b, h, i: (b, h, i, 0)),
        out_shape=jax.ShapeDtypeStruct((bsz, n_heads, lq, V_HEAD_DIM), BF16),
        scratch_shapes=[pltpu.VMEM((2, lk_all, 2 * qc), F32), pltpu.VMEM((2, V_HEAD_DIM, 2 * qc), F32),
                        pltpu.VMEM((2, 1, 2 * qc), F32), pltpu.VMEM((1, 2 * qc), F32)],
        compiler_params=_params("arbitrary", "arbitrary", "arbitrary"),
        name="diff_attention",
    )(*ins)


def _mlp_kernel(*refs, mix, mix_bias):
    refs = list(refs)
    if mix:
        y_ref, wo_ref = refs.pop(0), refs.pop(0)
        bo_ref = refs.pop(0) if mix_bias else None
        gmix_ref = refs.pop(0)
    h_ref, gpre_ref, gpost_ref, mod_ref, w1_ref, w2_ref, o_ref, hid_ref = refs
    tm = h_ref.shape[0]
    rt = tm // 2 if tm % (4 * SUBLANES) == 0 else tm
    dff = w1_ref.shape[1]
    fc = 4 * MXU_DIM

    def residual(rows):
        x = h_ref[rows, :]
        if mix:
            if len(y_ref.shape) == 4:
                yin = jnp.concatenate([y_ref[0, hd, rows, :] for hd in range(y_ref.shape[1])], axis=1)
            else:
                yin = y_ref[rows, :]
            y = _dot(yin, wo_ref[...])
            if mix_bias:
                y = y + bo_ref[...]
            x = x + mod_ref[0, 2:3, :] * _rms(y, gmix_ref[...])
        return x

    def hidden(rows, x):
        v = (_rms(x, gpre_ref[...]) * (1.0 + mod_ref[0, 4:5, :]) + mod_ref[0, 3:4, :]).astype(BF16)
        for c in range(dff // fc):
            hid = jnp.maximum(_dot(v, w1_ref[:, c * fc:(c + 1) * fc]), 0.0)
            hid_ref[rows, c * fc:(c + 1) * fc] = (hid * hid).astype(BF16)

    def output(rows, x):
        y = _dot(hid_ref[rows, :], w2_ref[...])
        o_ref[rows, :] = x + mod_ref[0, 5:6, :] * _rms(y, gpost_ref[...])

    halves = [slice(r0, r0 + rt) for r0 in range(0, tm, rt)]
    xs = [residual(rows) for rows in halves]
    for rows, x in zip(halves, xs):
        hidden(rows, x)
    for rows, x in zip(halves, xs):
        output(rows, x)


def _mlp_sublayer(h, gpre, gpost, mod, w1, w2, *, rows_per_mod, mix=None, tm=1024):
    rows, d = h.shape
    dff = w1.shape[1]
    head_major = mix is not None and mix[0].ndim == 4
    tm = _row_tile(min(rows, rows_per_mod, mix[0].shape[2] if head_major else rows), tm)
    tile = pl.BlockSpec((tm, d), lambda i: (i, 0))
    vec = pl.BlockSpec((1, d), lambda i: (0, 0))
    ins, specs, mix_bias = [], [], False
    if mix is not None:
        y, wo, bo, gmix = mix
        mix_bias = bo is not None
        ins = [y, wo] + ([bo.reshape(1, d)] if mix_bias else []) + [gmix.reshape(1, d)]
        if head_major:
            per = y.shape[2] // tm
            y_spec = pl.BlockSpec((1, y.shape[1], tm, V_HEAD_DIM), lambda i: (i // per, 0, i % per, 0))
        else:
            y_spec = tile
        specs = [y_spec, _resident((d, d), lambda i: (0, 0))] + ([vec] if mix_bias else []) + [vec]
    return pl.pallas_call(
        functools.partial(_mlp_kernel, mix=mix is not None, mix_bias=mix_bias),
        grid=(rows // tm,),
        in_specs=specs + [tile, vec, vec,
                          pl.BlockSpec((1, N_MOD, d), lambda i: (i * tm // rows_per_mod, 0, 0)),
                          _resident((d, dff), lambda i: (0, 0)),
                          _resident((dff, d), lambda i: (0, 0))],
        out_specs=tile,
        out_shape=jax.ShapeDtypeStruct((rows, d), F32),
        scratch_shapes=[pltpu.VMEM((tm, dff), BF16)],
        compiler_params=_params("arbitrary"),
        name="mlp_sublayer",
    )(*ins, h, gpre.reshape(1, d), gpost.reshape(1, d), mod, w1, w2)


def _hyena_filter_kernel(z_ref, w1_ref, b1_ref, w2_ref, b2_ref, fr_ref, wo_ref, dl_ref, hf_ref, hb_ref):
    hi = lax.Precision.HIGHEST
    z = z_ref[...]
    freq = fr_ref[...]
    hdn = jnp.sin(freq * (jnp.dot(z, w1_ref[...], precision=hi, preferred_element_type=F32) + b1_ref[...]))
    for j in range(w2_ref.shape[0]):
        hdn = jnp.sin(freq * (jnp.dot(hdn, w2_ref[j], precision=hi, preferred_element_type=F32) + b2_ref[j]))
    h = jnp.dot(hdn, wo_ref[...], precision=hi, preferred_element_type=F32)
    d = hf_ref.shape[1]
    window = jnp.exp(-z[:, 0:1] * dl_ref[...])
    hf_ref[...] = h[:, :d] * window
    hb_ref[...] = h[:, d:] * window


def _hyena_filters(n, w1, b1, w2, b2, freq, w_out, d):
    bands = (HYENA_EMB_DIM - 1) // 2
    t = jnp.linspace(0.0, 1.0, n, dtype=F32)[:, None]
    w = 2.0 * math.pi * jnp.arange(n, dtype=F32)[:, None] / n
    f = jnp.linspace(1e-4, bands - 1, bands, dtype=F32)[None, :]
    z = jnp.concatenate([t, jnp.cos(f * w), -jnp.sin(f * w),
                         jnp.zeros((n, LANES - HYENA_EMB_DIM), F32)], axis=-1)
    w1p = jnp.concatenate([w1, jnp.zeros((LANES - HYENA_EMB_DIM, w1.shape[1]), F32)], axis=0)
    deltas = jnp.abs(jnp.linspace(math.log(HYENA_DECAY_TARGET) / HYENA_SLOW_DECAY_PCT,
                                  math.log(HYENA_DECAY_TARGET) / HYENA_FAST_DECAY_PCT, d, dtype=F32))
    order = w1.shape[1]
    tr = _row_tile(n, 256)
    full = lambda *shape: pl.BlockSpec(shape, lambda i: (0,) * len(shape))
    return pl.pallas_call(
        _hyena_filter_kernel,
        grid=(n // tr,),
        in_specs=[pl.BlockSpec((tr, LANES), lambda i: (i, 0)),
                  full(LANES, order), full(1, order), full(*w2.shape), full(w2.shape[0], 1, order),
                  full(1, order), full(order, 2 * d), full(1, d)],
        out_specs=[pl.BlockSpec((tr, d), lambda i: (i, 0))] * 2,
        out_shape=[jax.ShapeDtypeStruct((n, d), F32)] * 2,
        compiler_params=_params("arbitrary"),
        name="hyena_filters",
    )(z, w1p, b1.reshape(1, order), w2, b2.reshape(w2.shape[0], 1, order), freq.reshape(1, order),
      w_out, deltas.reshape(1, d))


def _dft_matrices(n):
    k = jnp.arange(n, dtype=jnp.int32)
    blk = min(n, LANES)

    def tables(t):
        ang = ((k[:, None] * t[None, :]) % (2 * n)).astype(F32) * (math.pi / n)
        return jnp.cos(ang), jnp.sin(ang)

    ca, sa = tables(jnp.arange(n // blk, dtype=jnp.int32) * blk)
    cb, sb = tables(jnp.arange(blk, dtype=jnp.int32))
    cmat = ca[:, :, None] * cb[:, None, :] - sa[:, :, None] * sb[:, None, :]
    smat = sa[:, :, None] * cb[:, None, :] + ca[:, :, None] * sb[:, None, :]
    return cmat.reshape(n, n).astype(BF16), smat.reshape(n, n).astype(BF16)


def _alt_sign(n):
    row = lax.broadcasted_iota(jnp.int32, (n, 1), 0)
    return jnp.where(row % 2 == 0, 1.0, -1.0).astype(F32), row


def _hyena_spec_kernel(c_ref, s_ref, hf_ref, hb_ref, kr_ref, ks_ref, kn_ref):
    m = c_ref.shape[0]
    n = hf_ref.shape[0]
    nb = n // m
    sgn, row = _alt_sign(m)
    wk = jnp.where(row == 0, 0.5 / m, 1.0 / m)

    def transforms(seg):
        sb = seg.astype(BF16)
        first = sb[0:1, :].astype(F32)
        ct, st = _dot(c_ref[...], sb), _dot(s_ref[...], sb)
        at = jnp.sum(sgn * seg, axis=0, keepdims=True)
        return ct, st, at, ct - first, at - seg[0:1, :]

    fwd = [transforms(hf_ref[j * m:(j + 1) * m, :]) for j in range(nb)]
    bwd = [transforms(hb_ref[j * m:(j + 1) * m, :]) for j in range(nb)]
    for lag in range(-(nb - 1), nb):
        if lag == 0:
            kr, ks, kn = fwd[0][0] + bwd[0][3], fwd[0][1] - bwd[0][1], fwd[0][2] + bwd[0][4]
        else:
            cur, prv = (fwd[lag], fwd[lag - 1]) if lag > 0 else (bwd[-lag], bwd[-lag - 1])
            kr = cur[0] + sgn * prv[3]
            ks = cur[1] + sgn * prv[1]
            kn = cur[2] + prv[4]
            if lag < 0:
                ks = -ks
        kr_ref[lag + nb - 1] = kr * wk
        ks_ref[lag + nb - 1] = ks * wk
        kn_ref[lag + nb - 1] = kn * (0.5 / m)


def _hyena_spectrum(cmat, smat, hf, hb):
    n, d = hf.shape
    m = cmat.shape[0]
    assert n % m == 0 and m % 2 == 0
    nl = 2 * (n // m) - 1
    cb = min(d, MXU_DIM)
    return pl.pallas_call(
        _hyena_spec_kernel,
        grid=(d // cb,),
        in_specs=[_resident((m, m), lambda j: (0, 0)), _resident((m, m), lambda j: (0, 0)),
                  pl.BlockSpec((n, cb), lambda j: (0, j)), pl.BlockSpec((n, cb), lambda j: (0, j))],
        out_specs=[pl.BlockSpec((nl, m, cb), lambda j: (0, 0, j)), pl.BlockSpec((nl, m, cb), lambda j: (0, 0, j)),
                   pl.BlockSpec((nl, 1, cb), lambda j: (0, 0, j))],
        out_shape=[jax.ShapeDtypeStruct((nl, m, d), F32), jax.ShapeDtypeStruct((nl, m, d), F32),
                   jax.ShapeDtypeStruct((nl, 1, d), F32)],
        compiler_params=_params("arbitrary"),
        name="hyena_spectrum",
    )(cmat, smat, hf, hb)


def _hyena_conv_kernel(x0_ref, vv_ref, c_ref, s_ref, kr_ref, ks_ref, kn_ref, db_ref, o_ref,
                       vb_scr, fr_scr, fs_scr, p_scr, q_scr):
    n, cb = o_ref.shape[1], o_ref.shape[2]
    m = c_ref.shape[0]
    nb = n // m
    sgn, _ = _alt_sign(m)

    def transform(s):
        fn = []
        for jb in range(nb):
            blk = vv_ref[s, jb * m:(jb + 1) * m, :]
            vb_scr[s, :, jb * cb:(jb + 1) * cb] = blk
            fn.append(jnp.sum(blk.astype(F32) * sgn, axis=0, keepdims=True))
        fr_scr[s] = _dot(c_ref[...], vb_scr[s])
        fs_scr[s] = _dot(s_ref[...], vb_scr[s])
        return fn

    rc = min(m, 32)

    def spectral(s, fn):
        nyq = []
        for ib in range(nb):
            nyq.append(functools.reduce(jnp.add, [fn[jb] * kn_ref[ib - jb + nb - 1] for jb in range(nb)]))
            for r0 in range(0, m, rc):
                rows = slice(r0, r0 + rc)
                p = q = None
                for jb in range(nb):
                    cols = slice(jb * cb, (jb + 1) * cb)
                    fr, fs = fr_scr[s, rows, cols], fs_scr[s, rows, cols]
                    kr, ks = kr_ref[ib - jb + nb - 1, rows, :], ks_ref[ib - jb + nb - 1, rows, :]
                    dp, dq = fr * kr - fs * ks, fr * ks + fs * kr
                    p, q = (dp, dq) if p is None else (p + dp, q + dq)
                p_scr[s, rows, ib * cb:(ib + 1) * cb] = p.astype(BF16)
                q_scr[s, rows, ib * cb:(ib + 1) * cb] = q.astype(BF16)
        return nyq

    def inverse(s):
        return _dot(c_ref[...], p_scr[s]) + _dot(s_ref[...], q_scr[s])

    def emit(s, y, nyq):
        for ib in range(nb):
            rows = slice(ib * m, (ib + 1) * m)
            yb = y[:, ib * cb:(ib + 1) * cb] + sgn * nyq[ib] + vv_ref[s, rows, :].astype(F32) * db_ref[...]
            o_ref[s, rows, :] = (yb * x0_ref[s, rows, :].astype(F32)).astype(o_ref.dtype)

    seqs = range(o_ref.shape[0])
    fn = [transform(s) for s in seqs]
    y, nyq = [], []
    for s in seqs:
        nyq.append(spectral(s, fn[s]))
        y.append(inverse(s))
    for s in seqs:
        emit(s, y[s], nyq[s])


def _hyena_conv(x0, vv, cmat, smat, kr, ks, kn, d_bias):
    bsz, n, d = vv.shape
    cb = min(d, MXU_DIM)
    nb = d // cb
    m, nl = cmat.shape[0], kr.shape[0]
    ns = 1
    wide = (ns, m, (n // m) * cb)
    seq = pl.BlockSpec((ns, n, cb), lambda j, b: (b, 0, j))
    spec = pl.BlockSpec((nl, m, cb), lambda j, b: (0, 0, j))
    nyq = pl.BlockSpec((nl, 1, cb), lambda j, b: (0, 0, j))
    vec = pl.BlockSpec((1, cb), lambda j, b: (0, j))
    return pl.pallas_call(
        _hyena_conv_kernel,
        grid=(nb, bsz // ns),
        in_specs=[seq, seq, _resident((m, m), lambda j, b: (0, 0)), _resident((m, m), lambda j, b: (0, 0)),
                  spec, spec, nyq, vec],
        out_specs=seq,
        out_shape=jax.ShapeDtypeStruct((bsz, n, d), BF16),
        scratch_shapes=[pltpu.VMEM(wide, BF16), pltpu.VMEM(wide, F32), pltpu.VMEM(wide, F32),
                        pltpu.VMEM(wide, BF16), pltpu.VMEM(wide, BF16)],
        compiler_params=_params("arbitrary", "arbitrary"),
        name="hyena_conv",
    )(x0, vv, cmat, smat, kr, ks, kn, d_bias.reshape(1, d))


def _window_dft(nw):
    half = nw // 2
    t = jnp.arange(nw, dtype=jnp.int32)
    ang = ((jnp.arange(half, dtype=jnp.int32)[:, None] * t[None, :]) % nw).astype(F32) * (2.0 * math.pi / nw)
    nyq = jnp.where(t % 2 == 0, 1.0, -1.0).astype(F32)[None, :]
    return jnp.concatenate([jnp.cos(ang), nyq, jnp.sin(ang)[1:]], axis=0)


def _tap_spectrum_kernel(tr_ref, ts_ref, td_ref, w_ref, g_ref):
    hi = lax.Precision.HIGHEST
    for i, t_ref in enumerate((tr_ref, ts_ref, td_ref)):
        g_ref[i] = jnp.dot(t_ref[...], w_ref[...], precision=hi, preferred_element_type=F32)


def _tap_spectrum(w_dw, nw):
    width, d = w_dw.shape
    reach, half = width // 2, nw // 2
    kpad = -width % SUBLANES
    lag = reach - jnp.arange(width + kpad, dtype=jnp.int32)
    ang = ((jnp.arange(half, dtype=jnp.int32)[:, None] * lag[None, :]) % nw).astype(F32) * (2.0 * math.pi / nw)
    wf = jnp.where(jnp.arange(half) == 0, 1.0 / nw, 2.0 / nw).astype(F32)[:, None]
    tr = jnp.cos(ang) * wf
    ts = jnp.sin(ang) * wf
    nyq = jnp.where(lag % 2 == 0, 1.0 / nw, -1.0 / nw).astype(F32)[None, :]
    td = jnp.concatenate([nyq, tr[1:]], axis=0)
    wp = jnp.concatenate([w_dw, jnp.zeros((kpad, d), F32)], axis=0)
    full = lambda a: pl.BlockSpec(a.shape, lambda: (0,) * a.ndim)
    return pl.pallas_call(
        _tap_spectrum_kernel,
        in_specs=[full(tr), full(ts), full(td), full(wp)],
        out_specs=pl.BlockSpec((3, half, d), lambda: (0, 0, 0)),
        out_shape=jax.ShapeDtypeStruct((3, half, d), F32),
        name="tap_spectrum",
    )(tr, ts, td, wp)


def _conformer_post_kernel(x_ref, a_ref, ai_ref, gt_ref, bd_ref, lg_ref, lb_ref, w_ref, b_ref, h_ref, g_ref,
                           mod_ref, o_ref, xp_scr, ys_scr):
    n, d = x_ref.shape[1], x_ref.shape[2]
    nw = a_ref.shape[0]
    half = nw // 2
    tm = o_ref.shape[0]
    j = pl.program_id(1)

    @pl.when(j == 0)
    def _():
        xp_scr[0:CONV_HALO, :] = jnp.zeros((CONV_HALO, d), BF16)
        xp_scr[CONV_HALO:CONV_HALO + n, :] = x_ref[0]
        xp_scr[CONV_HALO + n:, :] = jnp.zeros((CONV_HALO, d), BF16)

    tw = nw - 2 * CONV_HALO
    base = pl.multiple_of(j * tm, tm)

    def forward(w):
        return _dot(a_ref[...], xp_scr[pl.ds(base + w * tw, nw), :])

    def spectral(w, f):
        xr, xs = f[:half], f[half:]
        ys_scr[w, 0:half, :] = (xr * gt_ref[0] - xs * gt_ref[1]).astype(BF16)
        ys_scr[w, half:, :] = (xr * gt_ref[1] + xs * gt_ref[2]).astype(BF16)

    def inverse(w):
        return _dot(ai_ref[...], ys_scr[w]) + bd_ref[...]

    def activate(z):
        mu = jnp.mean(z, axis=-1, keepdims=True)
        zc = z - mu
        var = jnp.mean(zc * zc, axis=-1, keepdims=True)
        zn = zc * lax.rsqrt(var + LN_EPS) * lg_ref[...] + lb_ref[...]
        return (zn / (1.0 + jnp.exp(-zn))).astype(BF16)

    def project(act):
        return _dot(act, w_ref[...]) + b_ref[...]

    def store(w, y):
        rows = slice(w * tw, (w + 1) * tw)
        o_ref[rows, :] = h_ref[rows, :] + mod_ref[0, 2:3, :] * _rms(y, g_ref[...])

    wins = range(tm // tw)
    f = [forward(w) for w in wins]
    z = []
    for w in wins:
        spectral(w, f[w])
        z.append(inverse(w))
    y = []
    for w in wins:
        y.append(project(activate(z[w])))
    for w in wins:
        store(w, y[w])


def _conformer_post(x, w_dw, b_dw, ln_g, ln_b, w, b, h, g, mod, *, per_batch_mod, tm=1024, tw=MXU_DIM):
    bsz, n, d = x.shape
    assert w_dw.shape[0] // 2 < CONV_HALO
    tm = _row_tile(n, tm)
    tw = _row_tile(tm, tw)
    per = n // tm
    nw = tw + 2 * CONV_HALO
    amat = _window_dft(nw)
    gtab = _tap_spectrum(w_dw, nw)
    vec = pl.BlockSpec((1, d), lambda bi, j: (0, 0))
    return pl.pallas_call(
        _conformer_post_kernel,
        grid=(bsz, per),
        in_specs=[pl.BlockSpec((1, n, d), lambda bi, j: (bi, 0, 0)),
                  _resident((nw, nw), lambda bi, j: (0, 0)), _resident((tw, nw), lambda bi, j: (0, 0)),
                  _resident((3, nw // 2, d), lambda bi, j: (0, 0, 0)), vec, vec, vec,
                  _resident((d, d), lambda bi, j: (0, 0)), vec,
                  pl.BlockSpec((tm, d), lambda bi, j: (bi * per + j, 0)), vec,
                  pl.BlockSpec((1, N_MOD, d), lambda bi, j: (bi if per_batch_mod else 0, 0, 0))],
        out_specs=pl.BlockSpec((tm, d), lambda bi, j: (bi * per + j, 0)),
        out_shape=jax.ShapeDtypeStruct((bsz * n, d), F32),
        scratch_shapes=[pltpu.VMEM((n + 2 * CONV_HALO, d), BF16), pltpu.VMEM((tm // tw, nw, d), BF16)],
        compiler_params=_params("arbitrary", "arbitrary"),
        name="conformer_post",
    )(x, amat.astype(BF16), amat.T[CONV_HALO:CONV_HALO + tw].astype(BF16), gtab, b_dw.reshape(1, d),
      ln_g.reshape(1, d), ln_b.reshape(1, d), w, b.reshape(1, d), h, g.reshape(1, d), mod)


def kernel(x, c, ctx, c_ctx, w_mod, b_mod, norm_mix_pre, norm_mix_post, norm_mlp_pre, norm_mlp_post, w_mlp_in, w_mlp_out, attn_w_qkv, attn_w_out, attn_lambda, attn_subln, hy_w_in, hy_b_in, hy_w_short, hy_b_short, hy_filt_w1, hy_filt_b1, hy_filt_w2, hy_filt_b2, hy_filt_freq, hy_filt_w_out, hy_bias, hy_w_out, hy_b_out, cv_w_pw1, cv_b_pw1, cv_w_dw, cv_b_dw, cv_ln_g, cv_ln_b, cv_w_pw2, cv_b_pw2):
    bsz, seq, d = x.shape
    n_ctx = ctx.shape[1]
    depth = w_mod.shape[0]
    assert bsz + 1 <= MOD_ROWS and d % (2 * MXU_DIM) == 0

    cc = jnp.concatenate([c, c_ctx[None, :], jnp.zeros((MOD_ROWS - bsz - 1, d), F32)], axis=0)
    mods = _mod_vectors(cc, w_mod, b_mod)
    rope_tabs = _rope_tables(seq)

    h_lat = x.reshape(bsz * seq, d)
    h_ctx = ctx.reshape(bsz * n_ctx, d)
    lat = dict(rows_per_mod=seq)
    cx = dict(rows_per_mod=bsz * n_ctx)
    for i in range(depth):
        last = i == depth - 1
        kind, j = i % N_MIXERS, i // N_MIXERS
        ctx_out = not last
        mod_l = mods[i, :bsz].reshape(bsz, N_MOD, d)
        mod_c = mods[i, bsz:bsz + 1].reshape(1, N_MOD, d)
        g_pre, g_post = norm_mix_pre[i], norm_mix_post[i]
        mix_l = mix_c = None
        if kind == 0:
            lam_init = 0.8 - 0.6 * math.exp(-0.3 * i)
            w_qkv, w_o = attn_w_qkv[j].astype(BF16), attn_w_out[j].astype(BF16)
            q_l, k_l, vt_l = _pre_project(h_lat, g_pre, mod_l, w_qkv, None, kind="attn", seq=seq,
                                          rope_tabs=rope_tabs, **lat)
            q_c, k_c, vt_c = _pre_project(h_ctx, g_pre, mod_c, w_qkv, None, kind="attn", seq=n_ctx, **cx)
            o_l = _diff_attention(q_l, [(k_c, vt_c), (k_l, vt_l)], attn_lambda[j], attn_subln[j], lam_init)
            mix_l = (o_l, w_o, None, g_post)
            if ctx_out:
                o_c = _diff_attention(q_c, [(k_c, vt_c)], attn_lambda[j], attn_subln[j], lam_init)
                mix_c = (o_c, w_o, None, g_post)
        elif kind == 1:
            w_in, w_o = hy_w_in[j].astype(BF16), hy_w_out[j].astype(BF16)

            def hyena(h, mod, n, rpm):
                x0, vv = _pre_project(h, g_pre, mod, w_in, hy_b_in[j], kind="hyena", rows_per_mod=rpm, seq=n,
                                      short=(hy_w_short[j], hy_b_short[j]))
                hf, hb = _hyena_filters(n, hy_filt_w1[j], hy_filt_b1[j], hy_filt_w2[j], hy_filt_b2[j],
                                        hy_filt_freq[j], hy_filt_w_out[j], d)
                cmat, smat = _dft_matrices(min(n, HYENA_BLOCK))
                kr, ks, kn = _hyena_spectrum(cmat, smat, hf, hb)
                g = _hyena_conv(x0.reshape(bsz, n, d), vv.reshape(bsz, n, d), cmat, smat, kr, ks, kn, hy_bias[j])
                return (g.reshape(bsz * n, d), w_o, hy_b_out[j], g_post)

            mix_l = hyena(h_lat, mod_l, seq, seq)
            if ctx_out:
                mix_c = hyena(h_ctx, mod_c, n_ctx, bsz * n_ctx)
        else:
            w_1, w_2 = cv_w_pw1[j].astype(BF16), cv_w_pw2[j].astype(BF16)

            def conformer(h, mod, n, rpm, per_batch_mod):
                a = _pre_project(h, g_pre, mod, w_1, cv_b_pw1[j], kind="conformer", rows_per_mod=rpm)
                return _conformer_post(a.reshape(bsz, n, d), cv_w_dw[j], cv_b_dw[j], cv_ln_g[j], cv_ln_b[j],
                                       w_2, cv_b_pw2[j], h, g_post, mod, per_batch_mod=per_batch_mod)

            h_lat = conformer(h_lat, mod_l, seq, seq, True)
            if ctx_out:
                h_ctx = conformer(h_ctx, mod_c, n_ctx, bsz * n_ctx, False)
        w1, w2 = w_mlp_in[i].astype(BF16), w_mlp_out[i].astype(BF16)
        h_lat = _mlp_sublayer(h_lat, norm_mlp_pre[i], norm_mlp_post[i], mod_l, w1, w2, mix=mix_l, **lat)
        if ctx_out:
            h_ctx = _mlp_sublayer(h_ctx, norm_mlp_pre[i], norm_mlp_post[i], mod_c, w1, w2, mix=mix_c, **cx)
    return h_lat.reshape(bsz, seq, d)
```
